```python
import math
import jax
import jax.numpy as jnp
from jax import lax
import numpy as np


D_MODEL = 1024
BATCH = 8
SEQ = 2048
DEPTH = 4

RMS_EPS = 1e-6
LN_EPS = 1e-5
NEG_INF = -1e30

NSA_HEADS = 8
NSA_KV_HEADS = 2
NSA_GROUP = NSA_HEADS // NSA_KV_HEADS
HEAD_DIM = 64
CMP_BLOCK = 32
CMP_STRIDE = 16
CMP_HIDDEN = 256
SEL_BLOCK = 64
SEL_TOPK = 16
WINDOW = 512
Q_BLOCK = 64
FORCE_BONUS = 1e4

SSM_HEADS = 8
SSM_HEAD_DIM = 64
SSM_D_INNER = SSM_HEADS * SSM_HEAD_DIM
SSM_GROUPS = 2
SSM_STATE = 128
SSM_CONV = 4
SSM_CHUNK = 128
SSM_BC_DIM = SSM_GROUPS * SSM_STATE
SSM_CONV_DIM = SSM_D_INNER + 2 * SSM_BC_DIM

GMLP_WIDTH = 2 * D_MODEL
GMLP_GROUPS = 8
GMLP_GROUP_DIM = GMLP_WIDTH // GMLP_GROUPS
GMLP_CHUNK = 128

FFN_HIDDEN = -(-8 * D_MODEL // (3 * 256)) * 256

NSA_Q_DIM = NSA_HEADS * HEAD_DIM
NSA_KV_DIM = NSA_KV_HEADS * HEAD_DIM
NSA_GATE_DIM = 3 * NSA_HEADS
EVEN_SPLITS = (NSA_Q_DIM,) + (NSA_KV_DIM,) * 6 + (NSA_GATE_DIM, SSM_D_INNER, SSM_CONV_DIM, SSM_HEADS)
EVEN_SPLIT_POINTS = [int(v) for v in np.cumsum(EVEN_SPLITS)[:-1]]
EVEN_IN_DIM = int(sum(EVEN_SPLITS))
EVEN_MIX_DIM = NSA_Q_DIM + SSM_D_INNER

kernel_name = 'hybrid_nsa_ssd_gmlp_trunk'


def rms_norm(x, w):
    xf = x.astype(jnp.float32)
    y = xf * lax.rsqrt(jnp.mean(jnp.square(xf), axis=-1, keepdims=True) + RMS_EPS)
    return (y * w.astype(jnp.float32)).astype(x.dtype)


def layer_norm(x, w, b):
    xf = x.astype(jnp.float32)
    mu = jnp.mean(xf, axis=-1, keepdims=True)
    var = jnp.mean(jnp.square(xf - mu), axis=-1, keepdims=True)
    y = (xf - mu) * lax.rsqrt(var + LN_EPS) * w.astype(jnp.float32) + b.astype(jnp.float32)
    return y.astype(x.dtype)


def masked_softmax(scores, mask):
    p = jax.nn.softmax(jnp.where(mask, scores, NEG_INF), axis=-1)
    return jnp.where(mask, p, 0.0)


def alibi_slopes(n):
    return (2.0 ** (-8.0 * np.arange(1, n + 1) / n)).astype(np.float32)


def compress_blocks(kv, pe, w1, w2):
    s = kv.shape[2]
    n_cmp = (s - CMP_BLOCK) // CMP_STRIDE + 1
    idx = np.arange(n_cmp)[:, None] * CMP_STRIDE + np.arange(CMP_BLOCK)[None, :]
    blocks = kv[:, :, idx] + pe
    blocks = blocks.reshape(blocks.shape[:3] + (CMP_BLOCK * HEAD_DIM,))
    return jax.nn.silu(blocks @ w1) @ w2


def selection_overlap(n_cmp, n_sel):
    cs = np.arange(n_cmp)[:, None] * CMP_STRIDE
    ss = np.arange(n_sel)[None, :] * SEL_BLOCK
    ov = np.minimum(cs + CMP_BLOCK, ss + SEL_BLOCK) - np.maximum(cs, ss)
    return (np.clip(ov, 0, None) / CMP_BLOCK).astype(np.float32)


def nsa_attention(q, kc, vc, ks, vs, kw, vw, gate_logits, cmp_pe, cmp_w1, cmp_w2):
    b, s = q.shape[0], q.shape[1]
    dtype = q.dtype
    f32 = jnp.float32
    g, r, hd = NSA_KV_HEADS, NSA_GROUP, HEAD_DIM
    qg = (q * (hd ** -0.5)).transpose(0, 2, 1, 3).reshape(b, g, r, s, hd)
    kc, vc, ks, vs, kw, vw = [t.transpose(0, 2, 1, 3) for t in (kc, vc, ks, vs, kw, vw)]
    slopes = jnp.asarray(alibi_slopes(NSA_HEADS).reshape(g, r))[:, :, None, None]
    pos = np.arange(s)

    kcb = compress_blocks(kc, cmp_pe[0], cmp_w1[0], cmp_w2[0])
    vcb = compress_blocks(vc, cmp_pe[1], cmp_w1[1], cmp_w2[1])
    n_cmp = kcb.shape[2]
    cmp_end = np.arange(n_cmp) * CMP_STRIDE + CMP_BLOCK - 1
    d_cmp = (pos[:, None] - cmp_end[None, :]).astype(np.float32)
    s_cmp = jnp.einsum('bgrtd,bgcd->bgrtc', qg, kcb).astype(f32) - slopes * d_cmp
    p_cmp = masked_softmax(s_cmp, d_cmp >= 0)
    o_cmp = jnp.einsum('bgrtc,bgcd->bgrtd', p_cmp.astype(dtype), vcb)

    n_sel = s // SEL_BLOCK
    top_k = min(SEL_TOPK, n_sel)
    imp = jnp.einsum('bgrtc,cj->bgtj', p_cmp, jnp.asarray(selection_overlap(n_cmp, n_sel)))
    blk = np.arange(n_sel)[None, :]
    cur = (pos // SEL_BLOCK)[:, None]
    forced = ((blk == 0) | (blk == cur) | (blk == cur - 1)).astype(np.float32) * FORCE_BONUS
    eligible = blk * SEL_BLOCK <= pos[:, None]
    imp = jnp.where(eligible, imp + forced, -1.0)
    _, sel_idx = lax.top_k(imp, top_k)

    nqb = s // Q_BLOCK
    q_blocks = qg.reshape(b, g, r, nqb, Q_BLOCK, hd).transpose(3, 0, 1, 2, 4, 5)
    idx_blocks = sel_idx.reshape(b, g, nqb, Q_BLOCK, top_k).transpose(2, 0, 1, 3, 4)
    ks_blk = ks.reshape(b, g, n_sel, SEL_BLOCK, hd)
    vs_blk = vs.reshape(b, g, n_sel, SEL_BLOCK, hd)
    pad = ((0, 0), (0, 0), (WINDOW, 0), (0, 0))
    kw_pad = jnp.pad(kw, pad)
    vw_pad = jnp.pad(vw, pad)
    gather_blocks = jax.vmap(jax.vmap(lambda kv, ix: kv[ix]))
    n_key_sel = top_k * SEL_BLOCK
    n_key_win = Q_BLOCK + WINDOW

    def query_block(args):
        qi, q_blk, ix = args
        start = qi * Q_BLOCK
        t = start + jnp.arange(Q_BLOCK)
        k_sel = gather_blocks(ks_blk, ix).reshape(b, g, Q_BLOCK, n_key_sel, hd)
        v_sel = gather_blocks(vs_blk, ix).reshape(b, g, Q_BLOCK, n_key_sel, hd)
        s_pos = (ix[..., None] * SEL_BLOCK + jnp.arange(SEL_BLOCK)).reshape(b, g, Q_BLOCK, n_key_sel)
        d_sel = (t[:, None] - s_pos)[:, :, None]
        s_sel = jnp.einsum('bgrqd,bgqkd->bgrqk', q_blk, k_sel).astype(f32) - slopes * d_sel.astype(f32)
        o_sel = jnp.einsum('bgrqk,bgqkd->bgrqd', masked_softmax(s_sel, d_sel >= 0).astype(dtype), v_sel)
        k_win = lax.dynamic_slice_in_dim(kw_pad, start, n_key_win, axis=2)
        v_win = lax.dynamic_slice_in_dim(vw_pad, start, n_key_win, axis=2)
        w_pos = start - WINDOW + jnp.arange(n_key_win)
        d_win = t[:, None] - w_pos[None, :]
        s_win = jnp.einsum('bgrqd,bgkd->bgrqk', q_blk, k_win).astype(f32) - slopes * d_win.astype(f32)
        win_mask = (d_win >= 0) & (d_win < WINDOW) & (w_pos >= 0)[None, :]
        o_win = jnp.einsum('bgrqk,bgkd->bgrqd', masked_softmax(s_win, win_mask).astype(dtype), v_win)
        return o_sel, o_win

    o_sel, o_win = lax.map(query_block, (jnp.arange(nqb), q_blocks, idx_blocks))
    o_sel = o_sel.transpose(1, 2, 3, 0, 4, 5).reshape(b, g, r, s, hd)
    o_win = o_win.transpose(1, 2, 3, 0, 4, 5).reshape(b, g, r, s, hd)

    gates = jax.nn.sigmoid(gate_logits.astype(f32)).astype(dtype).transpose(0, 2, 1, 3).reshape(b, g, r, s, 3)
    out = gates[..., 0:1] * o_cmp + gates[..., 1:2] * o_sel + gates[..., 2:3] * o_win
    return out.reshape(b, NSA_HEADS, s, hd).transpose(0, 2, 1, 3).reshape(b, s, NSA_HEADS * hd)


def causal_depthwise_conv(x, w, bias):
    ch = x.shape[-1]
    y = lax.conv_general_dilated(x, w[:, None, :].astype(x.dtype), window_strides=(1,),
                                 padding=[(SSM_CONV - 1, 0)], dimension_numbers=('NWC', 'WIO', 'NWC'),
                                 feature_group_count=ch)
    return y + bias.astype(x.dtype)


def ssd_chunked(x, a, bm, cm):
    b, s, h, p = x.shape
    g, n = bm.shape[2], bm.shape[3]
    r = h // g
    c, l = s // SSM_CHUNK, SSM_CHUNK
    x = x.reshape(b, c, l, g, r, p)
    bm = bm.reshape(b, c, l, g, n)
    cm = cm.reshape(b, c, l, g, n)
    a_cs = jnp.cumsum(a.reshape(b, c, l, g, r).transpose(0, 3, 4, 1, 2), axis=-1)
    causal = np.tril(np.ones((l, l), dtype=bool))
    decay_in = jnp.exp(jnp.where(causal, a_cs[..., :, None] - a_cs[..., None, :], -jnp.inf))
    cb = jnp.einsum('bclgn,bcsgn->bgcls', cm, bm)
    y_diag = jnp.einsum('bgrcls,bcsgrp->bclgrp', cb[:, :, None] * decay_in, x)
    decay_states = jnp.exp(a_cs[..., -1:] - a_cs)
    states = jnp.einsum('bcsgn,bgrcs,bcsgrp->bcgrpn', bm, decay_states, x)
    chunk_decay = jnp.exp(a_cs[..., -1])

    def step(carry, inp):
        st, dec = inp
        return carry * dec[..., None, None] + st, carry

    init = jnp.zeros((b, g, r, p, n), jnp.float32)
    _, prev = lax.scan(step, init, (states.transpose(1, 0, 2, 3, 4, 5), chunk_decay.transpose(3, 0, 1, 2)))
    prev = prev.transpose(1, 0, 2, 3, 4, 5)
    y_off = jnp.einsum('bclgn,bcgrpn,bgrcl->bclgrp', cm, prev, jnp.exp(a_cs))
    return (y_diag + y_off).reshape(b, s, h, p)


def mamba2_mixer(z, xbc, dt, conv_w, conv_b, dt_bias, a_log, d_skip, norm_w):
    b, s = z.shape[0], z.shape[1]
    dtype = z.dtype
    f32 = jnp.float32
    xbc = jax.nn.silu(causal_depthwise_conv(xbc, conv_w, conv_b))
    xs, bm, cm = jnp.split(xbc, [SSM_D_INNER, SSM_D_INNER + SSM_BC_DIM], axis=-1)
    xs = xs.reshape(b, s, SSM_HEADS, SSM_HEAD_DIM).astype(f32)
    bm = bm.reshape(b, s, SSM_GROUPS, SSM_STATE).astype(f32)
    cm = cm.reshape(b, s, SSM_GROUPS, SSM_STATE).astype(f32)
    dt = jax.nn.softplus(dt.astype(f32) + dt_bias.astype(f32))
    a = -jnp.exp(a_log.astype(f32))
    y = ssd_chunked(xs * dt[..., None], dt * a, bm, cm)
    y = y + xs * d_skip.astype(f32)[:, None]
    gsz = SSM_D_INNER // SSM_GROUPS
    y = y.reshape(b, s, SSM_GROUPS, gsz) * jax.nn.silu(z.astype(f32)).reshape(b, s, SSM_GROUPS, gsz)
    y = rms_norm(y, norm_w.reshape(SSM_GROUPS, gsz))
    return y.reshape(b, s, SSM_D_INNER).astype(dtype)


def attn_ssm_mixer(h, w_in, w_out, cmp_pe, cmp_w1, cmp_w2, conv_w, conv_b, dt_bias, a_log, d_skip, norm_w):
    b, s = h.shape[0], h.shape[1]
    q, kc, vc, ks, vs, kw, vw, gl, z, xbc, dt = jnp.split(h @ w_in, EVEN_SPLIT_POINTS, axis=-1)
    kv = [t.reshape(b, s, NSA_KV_HEADS, HEAD_DIM) for t in (kc, vc, ks, vs, kw, vw)]
    o_attn = nsa_attention(q.reshape(b, s, NSA_HEADS, HEAD_DIM), kv[0], kv[1], kv[2], kv[3], kv[4], kv[5],
                           gl.reshape(b, s, NSA_HEADS, 3), cmp_pe, cmp_w1, cmp_w2)
    o_ssm = mamba2_mixer(z, xbc, dt, conv_w, conv_b, dt_bias, a_log, d_skip, norm_w)
    return jnp.concatenate([o_attn, o_ssm], axis=-1) @ w_out


def chunked_gmlp(h, w_in, ln_w, ln_b, w_s, b_s, w_out):
    b, s = h.shape[0], h.shape[1]
    u, v = jnp.split(jax.nn.gelu(h @ w_in, approximate=False), 2, axis=-1)
    v = layer_norm(v, ln_w, ln_b).reshape(b, s // GMLP_CHUNK, GMLP_CHUNK, GMLP_GROUPS, GMLP_GROUP_DIM)
    causal = np.tril(np.ones((GMLP_CHUNK, GMLP_CHUNK), dtype=bool))
    ws = jnp.where(causal, w_s, jnp.zeros_like(w_s))
    mixed = jnp.einsum('gts,bcsgd->bctgd', ws, v) + b_s.T[:, :, None]
    return (u * mixed.reshape(b, s, GMLP_WIDTH)) @ w_out


def swiglu_ffn(h, w_gate, w_up, w_down):
    return (jax.nn.silu(h @ w_gate) * (h @ w_up)) @ w_down


def setup_inputs(seed: int = 0) -> dict:
    key = jax.random.key(seed)
    k = jax.random.split(key, 24)
    ne, no = (DEPTH + 1) // 2, DEPTH // 2
    f32 = jnp.float32

    def normal(kk, shape, scale):
        return scale * jax.random.normal(kk, shape, f32)

    dt0 = jnp.exp(jax.random.uniform(k[12], (ne, SSM_HEADS), f32, math.log(1e-3), math.log(1e-1)))
    return {
        'x': normal(k[0], (BATCH, SEQ, D_MODEL), 1.0),
        'norm_gains': 1.0 + normal(k[1], (DEPTH, 4, D_MODEL), 0.05),
        'ffn_w_gate': normal(k[2], (DEPTH, D_MODEL, FFN_HIDDEN), D_MODEL ** -0.5),
        'ffn_w_up': normal(k[3], (DEPTH, D_MODEL, FFN_HIDDEN), D_MODEL ** -0.5),
        'ffn_w_down': normal(k[4], (DEPTH, FFN_HIDDEN, D_MODEL), FFN_HIDDEN ** -0.5),
        'ev_w_in': normal(k[5], (ne, D_MODEL, EVEN_IN_DIM), D_MODEL ** -0.5),
        'ev_w_out': normal(k[6], (ne, EVEN_MIX_DIM, D_MODEL), EVEN_MIX_DIM ** -0.5),
        'nsa_cmp_pe': normal(k[7], (ne, 2, CMP_BLOCK, HEAD_DIM), 0.1),
        'nsa_cmp_w1': normal(k[8], (ne, 2, CMP_BLOCK * HEAD_DIM, CMP_HIDDEN), (CMP_BLOCK * HEAD_DIM) ** -0.5),
        'nsa_cmp_w2': normal(k[9], (ne, 2, CMP_HIDDEN, HEAD_DIM), CMP_HIDDEN ** -0.5),
        'ssm_conv_w': normal(k[10], (ne, SSM_CONV, SSM_CONV_DIM), SSM_CONV ** -0.5),
        'ssm_conv_b': normal(k[11], (ne, SSM_CONV_DIM), 0.02),
        'ssm_dt_bias': dt0 + jnp.log(-jnp.expm1(-dt0)),
        'ssm_a_log': jnp.log(jax.random.uniform(k[13], (ne, SSM_HEADS), f32, 1.0, 16.0)),
        'ssm_d': 1.0 + normal(k[14], (ne, SSM_HEADS), 0.1),
        'ssm_norm_w': 1.0 + normal(k[15], (ne, SSM_D_INNER), 0.05),
        'od_w_in': normal(k[16], (no, D_MODEL, 2 * GMLP_WIDTH), D_MODEL ** -0.5),
        'od_ln_w': 1.0 + normal(k[17], (no, GMLP_WIDTH), 0.05),
        'od_ln_b': normal(k[18], (no, GMLP_WIDTH), 0.02),
        'od_w_s': normal(k[19], (no, GMLP_GROUPS, GMLP_CHUNK, GMLP_CHUNK), GMLP_CHUNK ** -0.5),
        'od_b_s': 1.0 + normal(k[20], (no, GMLP_GROUPS, GMLP_CHUNK), 0.05),
        'od_w_out': normal(k[21], (no, GMLP_WIDTH, D_MODEL), GMLP_WIDTH ** -0.5),
    }


def reference(x, norm_gains, ffn_w_gate, ffn_w_up, ffn_w_down, ev_w_in, ev_w_out, nsa_cmp_pe, nsa_cmp_w1,
              nsa_cmp_w2, ssm_conv_w, ssm_conv_b, ssm_dt_bias, ssm_a_log, ssm_d, ssm_norm_w, od_w_in, od_ln_w,
              od_ln_b, od_w_s, od_b_s, od_w_out):
    h = x
    for layer in range(DEPTH):
        gains = norm_gains[layer]
        i = layer // 2
        y = rms_norm(h, gains[0])
        if layer % 2 == 0:
            y = attn_ssm_mixer(y, ev_w_in[i], ev_w_out[i], nsa_cmp_pe[i], nsa_cmp_w1[i], nsa_cmp_w2[i],
                               ssm_conv_w[i], ssm_conv_b[i], ssm_dt_bias[i], ssm_a_log[i], ssm_d[i], ssm_norm_w[i])
        else:
            y = chunked_gmlp(y, od_w_in[i], od_ln_w[i], od_ln_b[i], od_w_s[i], od_b_s[i], od_w_out[i])
        h = h + rms_norm(y, gains[1])
        y = swiglu_ffn(rms_norm(h, gains[2]), ffn_w_gate[layer], ffn_w_up[layer], ffn_w_down[layer])
        h = h + rms_norm(y, gains[3])
    return h
```

```python
import functools

import numpy as np
import jax
import jax.numpy as jnp
from jax import lax
from jax.experimental import pallas as pl
from jax.experimental.pallas import tpu as pltpu

F32 = jnp.float32
BF16 = jnp.bfloat16

D_MODEL = 1024
RMS_EPS = 1e-6
LN_EPS = 1e-5
NEG_INF = -1e30

NSA_HEADS = 8
NSA_KV_HEADS = 2
NSA_GROUP = NSA_HEADS // NSA_KV_HEADS
HEAD_DIM = 64
CMP_BLOCK = 32
CMP_STRIDE = 16
CMP_HIDDEN = 256
SEL_BLOCK = 64
SEL_TOPK = 16
WINDOW = 512
FORCE_BONUS = 1e4

SSM_HEADS = 8
SSM_HEAD_DIM = 64
SSM_D_INNER = SSM_HEADS * SSM_HEAD_DIM
SSM_GROUPS = 2
SSM_STATE = 128
SSM_CONV = 4
SSM_CHUNK = 128
SSM_BC_DIM = SSM_GROUPS * SSM_STATE
SSM_CONV_DIM = SSM_D_INNER + 2 * SSM_BC_DIM

GMLP_WIDTH = 2 * D_MODEL
GMLP_GROUPS = 8
GMLP_GROUP_DIM = GMLP_WIDTH // GMLP_GROUPS
GMLP_CHUNK = 128

LANES = 128
VMEM_LIMIT = 56 * 1024 * 1024


def _params(*sem):
    return pltpu.CompilerParams(dimension_semantics=sem, vmem_limit_bytes=VMEM_LIMIT)


def _rms(x, g):
    return x * lax.rsqrt(jnp.mean(x * x, axis=-1, keepdims=True) + RMS_EPS) * g


def _silu(x):
    return x * jax.nn.sigmoid(x)


def _dot(a, b):
    return jnp.dot(a, b, preferred_element_type=F32)


def _dot_nt(a, b):
    return lax.dot_general(a, b, (((1,), (1,)), ((), ())), preferred_element_type=F32)


def _dot_tn(a, b):
    return lax.dot_general(a, b, (((0,), (0,)), ((), ())), preferred_element_type=F32)


FFN_TM = 512
FFN_TH = 1408


def _ffn_body(h_ref, g_ref, wg_ref, wu_ref, wd_ref, o_ref, xn_ref, acc_ref):
    j = pl.program_id(1)

    @pl.when(j == 0)
    def _():
        xn_ref[...] = _rms(h_ref[...], g_ref[2:3, :]).astype(BF16)
        acc_ref[...] = jnp.zeros_like(acc_ref)

    xn = xn_ref[...]
    a = _dot(xn, wg_ref[...])
    b = _dot(xn, wu_ref[...])
    acc_ref[...] += _dot((_silu(a) * b).astype(BF16), wd_ref[...])

    @pl.when(j == pl.num_programs(1) - 1)
    def _():
        o_ref[...] = h_ref[...] + _rms(acc_ref[...], g_ref[3:4, :])


def _ffn(h, gains, wg, wu, wd, layer):
    t = h.shape[0]
    hidden = wg.shape[2]
    return pl.pallas_call(
        _ffn_body,
        grid=(t // FFN_TM, hidden // FFN_TH),
        in_specs=[
            pl.BlockSpec((FFN_TM, D_MODEL), lambda i, j: (i, 0)),
            pl.BlockSpec((None, 4, D_MODEL), lambda i, j: (layer, 0, 0)),
            pl.BlockSpec((None, D_MODEL, FFN_TH), lambda i, j: (layer, 0, j)),
            pl.BlockSpec((None, D_MODEL, FFN_TH), lambda i, j: (layer, 0, j)),
            pl.BlockSpec((None, FFN_TH, D_MODEL), lambda i, j: (layer, j, 0)),
        ],
        out_specs=pl.BlockSpec((FFN_TM, D_MODEL), lambda i, j: (i, 0)),
        out_shape=jax.ShapeDtypeStruct((t, D_MODEL), F32),
        scratch_shapes=[pltpu.VMEM((FFN_TM, D_MODEL), BF16), pltpu.VMEM((FFN_TM, D_MODEL), F32)],
        compiler_params=_params("parallel", "arbitrary"),
        name="ffn",
    )(h, gains, wg, wu, wd)


GMLP_TM = 256


def _gmlp_body(h_ref, g_ref, win_ref, lnw_ref, lnb_ref, ws_ref, bs_ref, wout_ref, o_ref, gate_ref):
    xn = _rms(h_ref[...], g_ref[0:1, :]).astype(BF16)
    uv = _dot(xn, win_ref[...])
    uv = 0.5 * uv * (1.0 + lax.erf(uv * np.float32(1.0 / np.sqrt(2.0))))
    u = uv[:, :GMLP_WIDTH]
    v = uv[:, GMLP_WIDTH:]
    mu = jnp.mean(v, axis=-1, keepdims=True)
    vc = v - mu
    var = jnp.mean(vc * vc, axis=-1, keepdims=True)
    v = (vc * lax.rsqrt(var + LN_EPS) * lnw_ref[...] + lnb_ref[...]).astype(BF16)
    row = lax.broadcasted_iota(jnp.int32, (GMLP_CHUNK, GMLP_CHUNK), 0)
    col = lax.broadcasted_iota(jnp.int32, (GMLP_CHUNK, GMLP_CHUNK), 1)
    causal = col <= row
    for g in range(GMLP_GROUPS):
        ws = jnp.where(causal, ws_ref[g], 0.0).astype(BF16)
        bias = bs_ref[:, g:g + 1]
        cols = slice(g * GMLP_GROUP_DIM, (g + 1) * GMLP_GROUP_DIM)
        for c in range(GMLP_TM // GMLP_CHUNK):
            rows = slice(c * GMLP_CHUNK, (c + 1) * GMLP_CHUNK)
            mixed = _dot(ws, v[rows, cols]) + bias
            gate_ref[rows, cols] = (u[rows, cols] * mixed).astype(BF16)
    y = _dot(gate_ref[...], wout_ref[...])
    o_ref[...] = h_ref[...] + _rms(y, g_ref[1:2, :])


def _gmlp(h, gains, w_in, ln_w, ln_b, w_s, b_s_t, w_out, layer, i):
    t = h.shape[0]
    const2 = lambda r: (i, 0, 0)
    return pl.pallas_call(
        _gmlp_body,
        grid=(t // GMLP_TM,),
        in_specs=[
            pl.BlockSpec((GMLP_TM, D_MODEL), lambda r: (r, 0)),
            pl.BlockSpec((None, 4, D_MODEL), lambda r: (layer, 0, 0)),
            pl.BlockSpec((None, D_MODEL, 2 * GMLP_WIDTH), const2),
            pl.BlockSpec((None, 1, GMLP_WIDTH), const2),
            pl.BlockSpec((None, 1, GMLP_WIDTH), const2),
            pl.BlockSpec((None, GMLP_GROUPS, GMLP_CHUNK, GMLP_CHUNK), lambda r: (i, 0, 0, 0)),
            pl.BlockSpec((None, GMLP_CHUNK, GMLP_GROUPS), const2),
            pl.BlockSpec((None, GMLP_WIDTH, D_MODEL), const2),
        ],
        out_specs=pl.BlockSpec((GMLP_TM, D_MODEL), lambda r: (r, 0)),
        out_shape=jax.ShapeDtypeStruct((t, D_MODEL), F32),
        scratch_shapes=[pltpu.VMEM((GMLP_TM, GMLP_WIDTH), BF16)],
        compiler_params=_params("parallel"),
        name="gmlp",
    )(h, gains, w_in, ln_w, ln_b, w_s, b_s_t, w_out)


PROJ_TM = 512
Q_AUG = NSA_HEADS * LANES
PROJ_SEGS = ((Q_AUG, BF16),
             (2 * LANES, F32),
             (4 * LANES, BF16),
             (SSM_D_INNER, F32),
             (SSM_CONV_DIM, F32),
             (LANES, F32),
             (SSM_D_INNER, F32))
PROJ_WIDTH = sum(w for w, _ in PROJ_SEGS)


def _prep_w_in(w):
    nq = NSA_HEADS * HEAD_DIM
    nkv = NSA_KV_HEADS * HEAD_DIM
    q = w[:, :nq].reshape(D_MODEL, NSA_HEADS, HEAD_DIM)
    grp = (np.arange(NSA_HEADS) // NSA_GROUP)[None, :, None]
    q_aug = jnp.concatenate([jnp.where(grp == g, q, 0.0) for g in range(NSA_KV_HEADS)], axis=-1)
    o = nq
    kv = w[:, o:o + 6 * nkv]
    o += 6 * nkv
    gl = w[:, o:o + 3 * NSA_HEADS]
    o += 3 * NSA_HEADS
    z = w[:, o:o + SSM_D_INNER]
    o += SSM_D_INNER
    xbc = w[:, o:o + SSM_CONV_DIM]
    o += SSM_CONV_DIM
    dt = w[:, o:o + SSM_HEADS]
    gldt = jnp.concatenate([gl, dt, jnp.zeros((D_MODEL, LANES - 3 * NSA_HEADS - SSM_HEADS), w.dtype)], axis=1)
    dte = jnp.repeat(dt, SSM_HEAD_DIM, axis=1)
    out = jnp.concatenate([q_aug.reshape(D_MODEL, Q_AUG), kv, z, xbc, gldt, dte], axis=1)
    assert out.shape[1] == PROJ_WIDTH
    return out.astype(BF16)


def _inproj_body(h_ref, g_ref, w_ref, *out_refs):
    xn = _rms(h_ref[...], g_ref[0:1, :]).astype(BF16)
    off = 0
    for ref in out_refs:
        n = ref.shape[-1]
        ref[...] = _dot(xn, w_ref[:, off:off + n]).astype(ref.dtype)
        off += n


def _inproj(h, gains, w, layer):
    t = h.shape[0]
    return pl.pallas_call(
        _inproj_body,
        grid=(t // PROJ_TM,),
        in_specs=[
            pl.BlockSpec((PROJ_TM, D_MODEL), lambda r: (r, 0)),
            pl.BlockSpec((None, 4, D_MODEL), lambda r: (layer, 0, 0)),
            pl.BlockSpec((D_MODEL, PROJ_WIDTH), lambda r: (0, 0)),
        ],
        out_specs=[pl.BlockSpec((PROJ_TM, n), lambda r: (r, 0)) for n, _ in PROJ_SEGS],
        out_shape=[jax.ShapeDtypeStruct((t, n), dt) for n, dt in PROJ_SEGS],
        compiler_params=_params("parallel"),
        name="inproj",
    )(h, gains, w)


N_CMP_PAD = 128
CMP_HALF = CMP_BLOCK // CMP_STRIDE


def _prep_cmp(pe, w1, w2):
    pe2 = jnp.concatenate([pe, pe], axis=-1)
    w1r = w1.reshape(2, CMP_HALF, CMP_STRIDE, HEAD_DIM, CMP_HIDDEN)
    w1d = jnp.concatenate([w1r, w1r], axis=3).reshape(2, CMP_HALF, CMP_STRIDE * LANES, CMP_HIDDEN)
    slots = []
    for kv in range(2):
        for g in range(NSA_KV_HEADS):
            slot = kv * NSA_KV_HEADS + g
            slots.append(jnp.pad(w2[kv], ((0, 0), (slot * HEAD_DIM, (3 - slot) * HEAD_DIM))))
    return pe2, w1d.astype(BF16), jnp.stack(slots).astype(BF16)


def _compress_body(kc_ref, vc_ref, pe_ref, w1_ref, w2_ref, o_ref):
    lane = lax.broadcasted_iota(jnp.int32, (N_CMP_PAD, LANES), 1)
    out = jnp.zeros((N_CMP_PAD, 4 * HEAD_DIM), F32)
    for kv in range(2):
        src = (kc_ref, vc_ref)[kv]
        pieces = [src[pl.ds(l, N_CMP_PAD, stride=CMP_STRIDE), :] for l in range(CMP_STRIDE)]
        for g in range(NSA_KV_HEADS):
            in_group = (lane >= g * HEAD_DIM) & (lane < (g + 1) * HEAD_DIM)
            pre = None
            for half in range(CMP_HALF):
                xs = [jnp.where(in_group, pieces[l] + pe_ref[kv, half * CMP_STRIDE + l:half * CMP_STRIDE + l + 1, :], 0.0)
                      for l in range(CMP_STRIDE)]
                xcat = jnp.concatenate(xs, axis=-1).astype(BF16)
                part = _dot(xcat, w1_ref[kv, half])
                if half == 1:
                    part = pltpu.roll(part, N_CMP_PAD - 1, 0)
                pre = part if pre is None else pre + part
            out = out + _dot(_silu(pre).astype(BF16), w2_ref[kv * NSA_KV_HEADS + g])
    o_ref[...] = out.astype(o_ref.dtype)


def _compress(kvc, pe2, w1d, w2p, b, s):
    return pl.pallas_call(
        _compress_body,
        grid=(b,),
        in_specs=[
            pl.BlockSpec((s, LANES), lambda r: (r, 0)),
            pl.BlockSpec((s, LANES), lambda r: (r, 1)),
            pl.BlockSpec(pe2.shape, lambda r: (0, 0, 0)),
            pl.BlockSpec(w1d.shape, lambda r: (0, 0, 0, 0)),
            pl.BlockSpec(w2p.shape, lambda r: (0, 0, 0)),
        ],
        out_specs=pl.BlockSpec((None, N_CMP_PAD, 4 * HEAD_DIM), lambda r: (r, 0, 0)),
        out_shape=jax.ShapeDtypeStruct((b, N_CMP_PAD, 4 * HEAD_DIM), BF16),
        compiler_params=_params("parallel"),
        name="compress",
    )(kvc, kvc, pe2, w1d, w2p)


ATT_T = 256
N_SEL_PAD = LANES
SEL_SHIFT = SEL_BLOCK.bit_length() - 1


def _att_consts(s):
    n_cmp = (s - CMP_BLOCK) // CMP_STRIDE + 1
    n_sel = s // SEL_BLOCK
    assert n_cmp <= N_CMP_PAD and n_sel <= N_SEL_PAD
    cs = np.arange(n_cmp)[:, None] * CMP_STRIDE
    ss = np.arange(n_sel)[None, :] * SEL_BLOCK
    ov = np.clip(np.minimum(cs + CMP_BLOCK, ss + SEL_BLOCK) - np.maximum(cs, ss), 0, None) / CMP_BLOCK
    ov_pad = np.zeros((N_CMP_PAD, N_SEL_PAD), np.float32)
    ov_pad[:n_cmp, :n_sel] = ov
    expand = np.zeros((N_SEL_PAD, s), np.float32)
    expand[np.arange(s) // SEL_BLOCK, np.arange(s)] = 1.0
    return jnp.asarray(ov_pad, BF16), jnp.asarray(expand, BF16)


def _softmax_tile(qs, k, v, valid, distf, slopes, m_ref, l_ref, acc_ref):
    for h in range(NSA_GROUP):
        sc = _dot_nt(qs[h], k) - slopes[h] * distf
        sc = jnp.where(valid, sc, NEG_INF)
        m_old = m_ref[h]
        m_new = jnp.maximum(m_old, jnp.max(sc, axis=-1, keepdims=True))
        alpha = jnp.exp(m_old - m_new)
        p = jnp.exp(sc - m_new)
        l_ref[h] = alpha * l_ref[h] + jnp.sum(p, axis=-1, keepdims=True)
        acc_ref[h] = alpha * acc_ref[h] + _dot(p.astype(BF16), v)
        m_ref[h] = m_new


def _attn_body(q_ref, kvcb_ref, ks_ref, vs_ref, kw_ref, vw_ref, gl_ref, ov_ref, ex_ref, o_ref,
               m_ref, l_ref, acc_ref, out_ref, *, n_sel):
    i = pl.program_id(1)
    t0 = i * ATT_T
    row = lax.broadcasted_iota(jnp.int32, (ATT_T, ATT_T), 0)
    col = lax.broadcasted_iota(jnp.int32, (ATT_T, ATT_T), 1)
    rel = row - col
    lane = lax.broadcasted_iota(jnp.int32, (ATT_T, LANES), 1)
    tpos = t0 + lax.broadcasted_iota(jnp.int32, (ATT_T, LANES), 0)
    gates = jax.nn.sigmoid(gl_ref[...])

    def reset():
        m_ref[...] = jnp.full(m_ref.shape, NEG_INF, F32)
        l_ref[...] = jnp.zeros(l_ref.shape, F32)
        acc_ref[...] = jnp.zeros(acc_ref.shape, F32)

    def flush(g, branch, first):
        for h in range(NSA_GROUP):
            hh = g * NSA_GROUP + h
            gate = gates[:, 3 * hh + branch:3 * hh + branch + 1]
            val = gate * (acc_ref[h] * (1.0 / l_ref[h]))
            out_ref[hh] = val if first else out_ref[hh] + val

    for g in range(NSA_KV_HEADS):
        slopes = [2.0 ** -(g * NSA_GROUP + h + 1) for h in range(NSA_GROUP)]
        qs = [q_ref[:, (g * NSA_GROUP + h) * LANES:(g * NSA_GROUP + h + 1) * LANES] * (HEAD_DIM ** -0.5)
              for h in range(NSA_GROUP)]

        kc = kvcb_ref[:, 0:LANES]
        vc = kvcb_ref[:, LANES:2 * LANES]
        d_cmp = tpos - (lane * CMP_STRIDE + (CMP_BLOCK - 1))
        valid_c = d_cmp >= 0
        d_cmp_f = d_cmp.astype(F32)
        psum = jnp.zeros((ATT_T, N_CMP_PAD), F32)
        for h in range(NSA_GROUP):
            hh = g * NSA_GROUP + h
            sc = _dot_nt(qs[h], kc) - slopes[h] * d_cmp_f
            sc = jnp.where(valid_c, sc, NEG_INF)
            e = jnp.exp(sc - jnp.max(sc, axis=-1, keepdims=True))
            p = e * (1.0 / jnp.sum(e, axis=-1, keepdims=True))
            p = jnp.where(valid_c, p, 0.0)
            psum = psum + p
            out_ref[hh] = gates[:, 3 * hh:3 * hh + 1] * _dot(p.astype(BF16), vc)

        p_hi = psum.astype(BF16)
        p_lo = (psum - p_hi.astype(F32)).astype(BF16)
        imp = _dot(p_hi, ov_ref[...]) + _dot(p_lo, ov_ref[...])
        cur = tpos >> SEL_SHIFT
        forced = (lane == 0) | (lane == cur) | (lane == cur - 1)
        eligible = lane * SEL_BLOCK <= tpos
        imp = jnp.where(eligible, imp + jnp.where(forced, FORCE_BONUS, 0.0), -1.0)
        rank = jnp.zeros((ATT_T, N_SEL_PAD), F32)
        for j in range(n_sel):
            cj = imp[:, j:j + 1]
            beats = (cj > imp) | ((cj == imp) & (lane > j))
            rank = rank + jnp.where(beats, 1.0, 0.0)
        sel = jnp.where((rank < min(SEL_TOPK, n_sel)) & (lane < n_sel), 1.0, 0.0).astype(BF16)

        reset()

        def sel_tile(kt, carry):
            k0 = pl.multiple_of(kt * ATT_T, ATT_T)
            k = ks_ref[pl.ds(k0, ATT_T), :]
            v = vs_ref[pl.ds(k0, ATT_T), :]
            chosen = _dot(sel, ex_ref[:, pl.ds(k0, ATT_T)])
            dist = rel + (t0 - k0)
            valid = (chosen > 0.5) & (dist >= 0)
            _softmax_tile(qs, k, v, valid, dist.astype(F32), slopes, m_ref, l_ref, acc_ref)
            return carry

        lax.fori_loop(0, i + 1, sel_tile, 0)
        flush(g, 1, False)

        reset()
        for back in range(WINDOW // ATT_T, -1, -1):
            @pl.when(i >= back)
            def _():
                k0 = pl.multiple_of((i - back) * ATT_T, ATT_T)
                k = kw_ref[pl.ds(k0, ATT_T), :]
                v = vw_ref[pl.ds(k0, ATT_T), :]
                dist = rel + back * ATT_T
                valid = (dist >= 0) & (dist < WINDOW)
                _softmax_tile(qs, k, v, valid, dist.astype(F32), slopes, m_ref, l_ref, acc_ref)
        flush(g, 2, False)

    for hh in range(NSA_HEADS):
        g = hh // NSA_GROUP
        o_ref[:, hh * HEAD_DIM:(hh + 1) * HEAD_DIM] = out_ref[hh][:, g * HEAD_DIM:(g + 1) * HEAD_DIM]


def _attention(q, kvcb, kvsw, gldt, ov, expand, b, s):
    nq = s // ATT_T
    n_sel = s // SEL_BLOCK
    kv_spec = lambda c: pl.BlockSpec((s, LANES), lambda r, i: (r, c))
    return pl.pallas_call(
        functools.partial(_attn_body, n_sel=n_sel),
        grid=(b, nq),
        in_specs=[
            pl.BlockSpec((ATT_T, Q_AUG), lambda r, i: (r * nq + i, 0)),
            pl.BlockSpec((None, N_CMP_PAD, 4 * HEAD_DIM), lambda r, i: (r, 0, 0)),
            kv_spec(0), kv_spec(1), kv_spec(2), kv_spec(3),
            pl.BlockSpec((ATT_T, LANES), lambda r, i: (r * nq + i, 0)),
            pl.BlockSpec(ov.shape, lambda r, i: (0, 0)),
            pl.BlockSpec(expand.shape, lambda r, i: (0, 0)),
        ],
        out_specs=pl.BlockSpec((ATT_T, NSA_HEADS * HEAD_DIM), lambda r, i: (r * nq + i, 0)),
        out_shape=jax.ShapeDtypeStruct((b * s, NSA_HEADS * HEAD_DIM), F32),
        scratch_shapes=[
            pltpu.VMEM((NSA_GROUP, ATT_T, 1), F32),
            pltpu.VMEM((NSA_GROUP, ATT_T, 1), F32),
            pltpu.VMEM((NSA_GROUP, ATT_T, LANES), F32),
            pltpu.VMEM((NSA_HEADS, ATT_T, LANES), F32),
        ],
        compiler_params=_params("parallel", "parallel"),
        name="nsa_attention",
    )(q, kvcb, kvsw, kvsw, kvsw, kvsw, gldt, ov, expand)


CONV_PAD = 8
DT_LANE = 3 * NSA_HEADS


def _split_dot(a, b):
    b0 = b.astype(BF16)
    r1 = b - b0.astype(F32)
    b1 = r1.astype(BF16)
    b2 = (r1 - b1.astype(F32)).astype(BF16)
    return _dot(a, b0) + _dot(a, b1) + _dot(a, b2)


def _split_dot_rhs(a, b):
    a0 = a.astype(BF16)
    r1 = a - a0.astype(F32)
    a1 = r1.astype(BF16)
    a2 = (r1 - a1.astype(F32)).astype(BF16)
    return _dot(a0, b) + _dot(a1, b) + _dot(a2, b)


def _softplus(x):
    return jnp.maximum(x, 0.0) + jnp.log1p(jnp.exp(-jnp.abs(x)))


def _ssd_body(xbc_ref, z_ref, gldt_ref, dte_ref, cw_ref, cb_ref, dtb_e_ref, alog_e_ref, dtb_c_ref, alog_c_ref,
              d_ref, nw_ref, o_ref, state_ref, xe_ref):
    c = pl.program_id(1)
    L = SSM_CHUNK

    @pl.when(c == 0)
    def _():
        state_ref[...] = jnp.zeros(state_ref.shape, F32)
        xe_ref[0:CONV_PAD, :] = jnp.zeros((CONV_PAD, SSM_CONV_DIM), F32)

    x_raw = xbc_ref[...]
    xe_ref[CONV_PAD:CONV_PAD + L, :] = x_raw
    acc = jnp.zeros((L, SSM_CONV_DIM), F32)
    for k in range(SSM_CONV):
        lo = CONV_PAD - (SSM_CONV - 1) + k
        acc = acc + xe_ref[lo:lo + L, :] * cw_ref[k:k + 1, :]
    xe_ref[0:CONV_PAD, :] = x_raw[L - CONV_PAD:, :]
    xa = _silu(acc + cb_ref[...])
    xs = xa[:, :SSM_D_INNER]
    bm = xa[:, SSM_D_INNER:SSM_D_INNER + SSM_BC_DIM]
    cm = xa[:, SSM_D_INNER + SSM_BC_DIM:]

    row = lax.broadcasted_iota(jnp.int32, (L, L), 0)
    col = lax.broadcasted_iota(jnp.int32, (L, L), 1)
    causal = col <= row
    tri = jnp.where(causal, 1.0, 0.0).astype(BF16)
    tri_t = jnp.where(row <= col, 1.0, 0.0).astype(BF16)

    dt = _softplus(dte_ref[...] + dtb_e_ref[...])
    a = dt * (-jnp.exp(alog_e_ref[...]))
    a_cs = _split_dot(tri, a)
    a_last = a_cs[L - 1:L, :]
    dt_t = _softplus(gldt_ref[...].T[DT_LANE:DT_LANE + SSM_HEADS, :] + dtb_c_ref[...])
    a_cs_t = _split_dot_rhs(dt_t * (-jnp.exp(alog_c_ref[...])), tri_t)

    xdt = xs * dt
    xw = (xdt * jnp.exp(a_last - a_cs)).astype(BF16)
    exp_cs = jnp.exp(a_cs)
    lane = lax.broadcasted_iota(jnp.int32, (L, LANES), 1)
    rpg = SSM_HEADS // SSM_GROUPS
    gw = rpg * SSM_HEAD_DIM
    ys = []
    for g in range(SSM_GROUPS):
        bm_g = bm[:, g * SSM_STATE:(g + 1) * SSM_STATE]
        cm_g = cm[:, g * SSM_STATE:(g + 1) * SSM_STATE].astype(BF16)
        cb = _dot_nt(cm_g, bm_g.astype(BF16))
        prev = state_ref[:, g * gw:(g + 1) * gw]
        y_off = _dot(cm_g, prev.astype(BF16)) * exp_cs[:, g * gw:(g + 1) * gw]
        new_state = _dot(bm_g.T.astype(BF16), xw[:, g * gw:(g + 1) * gw])
        state_ref[:, g * gw:(g + 1) * gw] = prev * jnp.exp(a_last[:, g * gw:(g + 1) * gw]) + new_state
        for pair in range(rpg // 2):
            h0 = g * rpg + 2 * pair
            x_pair = xdt[:, h0 * SSM_HEAD_DIM:(h0 + 2) * SSM_HEAD_DIM].astype(BF16)
            halves = []
            for h in (h0, h0 + 1):
                diff = a_cs[:, h * SSM_HEAD_DIM:h * SSM_HEAD_DIM + 1] - a_cs_t[h:h + 1, :]
                decay = jnp.exp(jnp.where(causal, diff, NEG_INF))
                halves.append(_dot((cb * decay).astype(BF16), x_pair))
            ys.append(jnp.where(lane < SSM_HEAD_DIM, halves[0], halves[1])
                      + y_off[:, 2 * pair * SSM_HEAD_DIM:(2 * pair + 2) * SSM_HEAD_DIM])
    y = jnp.concatenate(ys, axis=-1) + xs * d_ref[...]
    y = y * _silu(z_ref[...])
    outs = []
    for g in range(SSM_GROUPS):
        yg = y[:, g * gw:(g + 1) * gw]
        outs.append(_rms(yg, nw_ref[:, g * gw:(g + 1) * gw]))
    o_ref[...] = jnp.concatenate(outs, axis=-1)


def _ssd(xbc, z, gldt, dte, conv_w, conv_b, dtb_e, alog_e, dtb_c, alog_c, d_e, norm_w, b, s):
    nc = s // SSM_CHUNK
    rows = lambda n: pl.BlockSpec((SSM_CHUNK, n), lambda r, c: (r * nc + c, 0))
    full = lambda arr: pl.BlockSpec(arr.shape, lambda r, c: (0,) * arr.ndim)
    consts = (conv_w, conv_b, dtb_e, alog_e, dtb_c, alog_c, d_e, norm_w)
    return pl.pallas_call(
        _ssd_body,
        grid=(b, nc),
        in_specs=[rows(SSM_CONV_DIM), rows(SSM_D_INNER), rows(LANES), rows(SSM_D_INNER)] + [full(a) for a in consts],
        out_specs=rows(SSM_D_INNER),
        out_shape=jax.ShapeDtypeStruct((b * s, SSM_D_INNER), F32),
        scratch_shapes=[
            pltpu.VMEM((SSM_STATE, SSM_D_INNER), F32),
            pltpu.VMEM((CONV_PAD + SSM_CHUNK, SSM_CONV_DIM), F32),
        ],
        compiler_params=_params("parallel", "arbitrary"),
        name="ssd",
    )(xbc, z, gldt, dte, *consts)


def _outproj_body(h_ref, oa_ref, os_ref, g_ref, w_ref, o_ref):
    na = oa_ref.shape[-1]
    y = _dot(oa_ref[...].astype(BF16), w_ref[:na, :]) + _dot(os_ref[...].astype(BF16), w_ref[na:, :])
    o_ref[...] = h_ref[...] + _rms(y, g_ref[1:2, :])


def _outproj(h, oa, os_, gains, w, layer):
    t = h.shape[0]
    rows = lambda n: pl.BlockSpec((PROJ_TM, n), lambda r: (r, 0))
    return pl.pallas_call(
        _outproj_body,
        grid=(t // PROJ_TM,),
        in_specs=[
            rows(D_MODEL), rows(oa.shape[1]), rows(os_.shape[1]),
            pl.BlockSpec((None, 4, D_MODEL), lambda r: (layer, 0, 0)),
            pl.BlockSpec(w.shape, lambda r: (0, 0)),
        ],
        out_specs=rows(D_MODEL),
        out_shape=jax.ShapeDtypeStruct((t, D_MODEL), F32),
        compiler_params=_params("parallel"),
        name="outproj",
    )(h, oa, os_, gains, w)


def _even_mixer(h, gains, layer, w_in, w_out, pe, w1, w2, conv_w, conv_b, dt_bias, a_log, d_skip, norm_w, b, s):
    q, kvc, kvsw, z, xbc, gldt, dte = _inproj(h, gains, _prep_w_in(w_in), layer)
    pe2, w1d, w2p = _prep_cmp(pe, w1, w2)
    kvcb = _compress(kvc, pe2, w1d, w2p, b, s)
    ov, expand = _att_consts(s)
    o_attn = _attention(q, kvcb, kvsw, gldt, ov, expand, b, s)
    rep = lambda v: jnp.repeat(v, SSM_HEAD_DIM)[None, :]
    o_ssm = _ssd(xbc, z, gldt, dte, conv_w, conv_b[None, :], rep(dt_bias), rep(a_log), dt_bias[:, None],
                 a_log[:, None], rep(d_skip), norm_w[None, :], b, s)
    return _outproj(h, o_attn, o_ssm, gains, w_out.astype(BF16), layer)


def kernel(x, norm_gains, ffn_w_gate, ffn_w_up, ffn_w_down, ev_w_in, ev_w_out, nsa_cmp_pe, nsa_cmp_w1, nsa_cmp_w2,
           ssm_conv_w, ssm_conv_b, ssm_dt_bias, ssm_a_log, ssm_d, ssm_norm_w, od_w_in, od_ln_w, od_ln_b, od_w_s,
           od_b_s, od_w_out):
    b, s, d = x.shape
    depth = norm_gains.shape[0]
    h = x.reshape(b * s, d)
    wg = ffn_w_gate.astype(BF16)
    wu = ffn_w_up.astype(BF16)
    wd = ffn_w_down.astype(BF16)
    od_in = od_w_in.astype(BF16)
    od_out = od_w_out.astype(BF16)
    od_lnw = od_ln_w[:, None, :]
    od_lnb = od_ln_b[:, None, :]
    od_bst = jnp.swapaxes(od_b_s, 1, 2)
    for layer in range(depth):
        i = layer // 2
        if layer % 2 == 0:
            h = _even_mixer(h, norm_gains, layer, ev_w_in[i], ev_w_out[i], nsa_cmp_pe[i], nsa_cmp_w1[i],
                            nsa_cmp_w2[i], ssm_conv_w[i], ssm_conv_b[i], ssm_dt_bias[i], ssm_a_log[i], ssm_d[i],
                            ssm_norm_w[i], b, s)
        else:
            h = _gmlp(h, norm_gains, od_in, od_lnw, od_lnb, od_w_s, od_bst, od_out, layer, i)
        h = _ffn(h, norm_gains, wg, wu, wd, layer)
    return h.reshape(b, s, d)
```

```python
import functools

import numpy as np
import jax
import jax.numpy as jnp
from jax import lax
from jax.experimental import pallas as pl
from jax.experimental.pallas import tpu as pltpu

F32 = jnp.float32
BF16 = jnp.bfloat16

D_MODEL = 1024
RMS_EPS = 1e-6
LN_EPS = 1e-5
NEG_INF = -1e30

NSA_HEADS = 8
NSA_KV_HEADS = 2
NSA_GROUP = NSA_HEADS // NSA_KV_HEADS
HEAD_DIM = 64
CMP_BLOCK = 32
CMP_STRIDE = 16
CMP_HIDDEN = 256
SEL_BLOCK = 64
SEL_TOPK = 16
WINDOW = 512
FORCE_BONUS = 1e4

SSM_HEADS = 8
SSM_HEAD_DIM = 64
SSM_D_INNER = SSM_HEADS * SSM_HEAD_DIM
SSM_GROUPS = 2
SSM_STATE = 128
SSM_CONV = 4
SSM_CHUNK = 128
SSM_BC_DIM = SSM_GROUPS * SSM_STATE
SSM_CONV_DIM = SSM_D_INNER + 2 * SSM_BC_DIM

GMLP_WIDTH = 2 * D_MODEL
GMLP_GROUPS = 8
GMLP_GROUP_DIM = GMLP_WIDTH // GMLP_GROUPS
GMLP_CHUNK = 128

LANES = 128
VMEM_LIMIT = 56 * 1024 * 1024


def _params(*sem):
    return pltpu.CompilerParams(dimension_semantics=sem, vmem_limit_bytes=VMEM_LIMIT)


def _rms(x, g):
    return x * lax.rsqrt(jnp.mean(x * x, axis=-1, keepdims=True) + RMS_EPS) * g


def _silu(x):
    return x * jax.nn.sigmoid(x)


def _dot(a, b):
    return jnp.dot(a, b, preferred_element_type=F32)


def _dot_nt(a, b):
    return lax.dot_general(a, b, (((1,), (1,)), ((), ())), preferred_element_type=F32)


def _dot_tn(a, b):
    return lax.dot_general(a, b, (((0,), (0,)), ((), ())), preferred_element_type=F32)


FFN_TM = 512
FFN_TH = 1408


def _ffn_body(h_ref, g_ref, wg_ref, wu_ref, wd_ref, o_ref, xn_ref, acc_ref):
    j = pl.program_id(1)

    @pl.when(j == 0)
    def _():
        xn_ref[...] = _rms(h_ref[...], g_ref[2:3, :]).astype(BF16)
        acc_ref[...] = jnp.zeros_like(acc_ref)

    xn = xn_ref[...]
    a = _dot(xn, wg_ref[...])
    b = _dot(xn, wu_ref[...])
    acc_ref[...] += _dot((_silu(a) * b).astype(BF16), wd_ref[...])

    @pl.when(j == pl.num_programs(1) - 1)
    def _():
        o_ref[...] = h_ref[...] + _rms(acc_ref[...], g_ref[3:4, :])


def _ffn(h, gains, wg, wu, wd, layer):
    t = h.shape[0]
    hidden = wg.shape[2]
    return pl.pallas_call(
        _ffn_body,
        grid=(t // FFN_TM, hidden // FFN_TH),
        in_specs=[
            pl.BlockSpec((FFN_TM, D_MODEL), lambda i, j: (i, 0)),
            pl.BlockSpec((None, 4, D_MODEL), lambda i, j: (layer, 0, 0)),
            pl.BlockSpec((None, D_MODEL, FFN_TH), lambda i, j: (layer, 0, j)),
            pl.BlockSpec((None, D_MODEL, FFN_TH), lambda i, j: (layer, 0, j)),
            pl.BlockSpec((None, FFN_TH, D_MODEL), lambda i, j: (layer, j, 0)),
        ],
        out_specs=pl.BlockSpec((FFN_TM, D_MODEL), lambda i, j: (i, 0)),
        out_shape=jax.ShapeDtypeStruct((t, D_MODEL), F32),
        scratch_shapes=[pltpu.VMEM((FFN_TM, D_MODEL), BF16), pltpu.VMEM((FFN_TM, D_MODEL), F32)],
        compiler_params=_params("parallel", "arbitrary"),
        name="ffn",
    )(h, gains, wg, wu, wd)


GMLP_TM = 256


def _gmlp_body(h_ref, g_ref, win_ref, lnw_ref, lnb_ref, ws_ref, bs_ref, wout_ref, o_ref, gate_ref):
    xn = _rms(h_ref[...], g_ref[0:1, :]).astype(BF16)
    uv = _dot(xn, win_ref[...])
    uv = 0.5 * uv * (1.0 + lax.erf(uv * np.float32(1.0 / np.sqrt(2.0))))
    u = uv[:, :GMLP_WIDTH]
    v = uv[:, GMLP_WIDTH:]
    mu = jnp.mean(v, axis=-1, keepdims=True)
    vc = v - mu
    var = jnp.mean(vc * vc, axis=-1, keepdims=True)
    v = (vc * lax.rsqrt(var + LN_EPS) * lnw_ref[...] + lnb_ref[...]).astype(BF16)
    row = lax.broadcasted_iota(jnp.int32, (GMLP_CHUNK, GMLP_CHUNK), 0)
    col = lax.broadcasted_iota(jnp.int32, (GMLP_CHUNK, GMLP_CHUNK), 1)
    causal = col <= row
    for g in range(GMLP_GROUPS):
        ws = jnp.where(causal, ws_ref[g], 0.0).astype(BF16)
        bias = bs_ref[:, g:g + 1]
        cols = slice(g * GMLP_GROUP_DIM, (g + 1) * GMLP_GROUP_DIM)
        for c in range(GMLP_TM // GMLP_CHUNK):
            rows = slice(c * GMLP_CHUNK, (c + 1) * GMLP_CHUNK)
            mixed = _dot(ws, v[rows, cols]) + bias
            gate_ref[rows, cols] = (u[rows, cols] * mixed).astype(BF16)
    y = _dot(gate_ref[...], wout_ref[...])
    o_ref[...] = h_ref[...] + _rms(y, g_ref[1:2, :])


def _gmlp(h, gains, w_in, ln_w, ln_b, w_s, b_s_t, w_out, layer, i):
    t = h.shape[0]
    const2 = lambda r: (i, 0, 0)
    return pl.pallas_call(
        _gmlp_body,
        grid=(t // GMLP_TM,),
        in_specs=[
            pl.BlockSpec((GMLP_TM, D_MODEL), lambda r: (r, 0)),
            pl.BlockSpec((None, 4, D_MODEL), lambda r: (layer, 0, 0)),
            pl.BlockSpec((None, D_MODEL, 2 * GMLP_WIDTH), const2),
            pl.BlockSpec((None, 1, GMLP_WIDTH), const2),
            pl.BlockSpec((None, 1, GMLP_WIDTH), const2),
            pl.BlockSpec((None, GMLP_GROUPS, GMLP_CHUNK, GMLP_CHUNK), lambda r: (i, 0, 0, 0)),
            pl.BlockSpec((None, GMLP_CHUNK, GMLP_GROUPS), const2),
            pl.BlockSpec((None, GMLP_WIDTH, D_MODEL), const2),
        ],
        out_specs=pl.BlockSpec((GMLP_TM, D_MODEL), lambda r: (r, 0)),
        out_shape=jax.ShapeDtypeStruct((t, D_MODEL), F32),
        scratch_shapes=[pltpu.VMEM((GMLP_TM, GMLP_WIDTH), BF16)],
        compiler_params=_params("parallel"),
        name="gmlp",
    )(h, gains, w_in, ln_w, ln_b, w_s, b_s_t, w_out)


PROJ_TM = 512
Q_AUG = NSA_HEADS * LANES
PROJ_SEGS = ((Q_AUG, BF16),
             (2 * LANES, F32),
             (4 * LANES, BF16),
             (SSM_D_INNER, F32),
             (SSM_CONV_DIM, F32),
             (LANES, F32),
             (SSM_D_INNER, F32))
PROJ_WIDTH = sum(w for w, _ in PROJ_SEGS)


def _prep_w_in(w):
    nq = NSA_HEADS * HEAD_DIM
    nkv = NSA_KV_HEADS * HEAD_DIM
    q = w[:, :nq].reshape(D_MODEL, NSA_HEADS, HEAD_DIM)
    grp = (np.arange(NSA_HEADS) // NSA_GROUP)[None, :, None]
    q_aug = jnp.concatenate([jnp.where(grp == g, q, 0.0) for g in range(NSA_KV_HEADS)], axis=-1)
    o = nq
    kv = w[:, o:o + 6 * nkv]
    o += 6 * nkv
    gl = w[:, o:o + 3 * NSA_HEADS]
    o += 3 * NSA_HEADS
    z = w[:, o:o + SSM_D_INNER]
    o += SSM_D_INNER
    xbc = w[:, o:o + SSM_CONV_DIM]
    o += SSM_CONV_DIM
    dt = w[:, o:o + SSM_HEADS]
    gldt = jnp.concatenate([gl, dt, jnp.zeros((D_MODEL, LANES - 3 * NSA_HEADS - SSM_HEADS), w.dtype)], axis=1)
    dte = jnp.repeat(dt, SSM_HEAD_DIM, axis=1)
    out = jnp.concatenate([q_aug.reshape(D_MODEL, Q_AUG), kv, z, xbc, gldt, dte], axis=1)
    assert out.shape[1] == PROJ_WIDTH
    return out.astype(BF16)


def _inproj_body(h_ref, g_ref, w_ref, *out_refs):
    xn = _rms(h_ref[...], g_ref[0:1, :]).astype(BF16)
    off = 0
    for ref in out_refs:
        n = ref.shape[-1]
        ref[...] = _dot(xn, w_ref[:, off:off + n]).astype(ref.dtype)
        off += n


def _inproj(h, gains, w, layer):
    t = h.shape[0]
    return pl.pallas_call(
        _inproj_body,
        grid=(t // PROJ_TM,),
        in_specs=[
            pl.BlockSpec((PROJ_TM, D_MODEL), lambda r: (r, 0)),
            pl.BlockSpec((None, 4, D_MODEL), lambda r: (layer, 0, 0)),
            pl.BlockSpec((D_MODEL, PROJ_WIDTH), lambda r: (0, 0)),
        ],
        out_specs=[pl.BlockSpec((PROJ_TM, n), lambda r: (r, 0)) for n, _ in PROJ_SEGS],
        out_shape=[jax.ShapeDtypeStruct((t, n), dt) for n, dt in PROJ_SEGS],
        compiler_params=_params("parallel"),
        name="inproj",
    )(h, gains, w)


N_CMP_PAD = 128
CMP_HALF = CMP_BLOCK // CMP_STRIDE


def _prep_cmp(pe, w1, w2):
    pe2 = jnp.concatenate([pe, pe], axis=-1)
    w1r = w1.reshape(2, CMP_HALF, CMP_STRIDE, HEAD_DIM, CMP_HIDDEN)
    w1d = jnp.concatenate([w1r, w1r], axis=3).reshape(2, CMP_HALF, CMP_STRIDE * LANES, CMP_HIDDEN)
    slots = []
    for kv in range(2):
        for g in range(NSA_KV_HEADS):
            slot = kv * NSA_KV_HEADS + g
            slots.append(jnp.pad(w2[kv], ((0, 0), (slot * HEAD_DIM, (3 - slot) * HEAD_DIM))))
    return pe2, w1d.astype(BF16), jnp.stack(slots).astype(BF16)


def _compress_body(kc_ref, vc_ref, pe_ref, w1_ref, w2_ref, o_ref):
    lane = lax.broadcasted_iota(jnp.int32, (N_CMP_PAD, LANES), 1)
    out = jnp.zeros((N_CMP_PAD, 4 * HEAD_DIM), F32)
    for kv in range(2):
        src = (kc_ref, vc_ref)[kv]
        pieces = [src[pl.ds(l, N_CMP_PAD, stride=CMP_STRIDE), :] for l in range(CMP_STRIDE)]
        for g in range(NSA_KV_HEADS):
            in_group = (lane >= g * HEAD_DIM) & (lane < (g + 1) * HEAD_DIM)
            pre = None
            for half in range(CMP_HALF):
                xs = [jnp.where(in_group, pieces[l] + pe_ref[kv, half * CMP_STRIDE + l:half * CMP_STRIDE + l + 1, :], 0.0)
                      for l in range(CMP_STRIDE)]
                xcat = jnp.concatenate(xs, axis=-1).astype(BF16)
                part = _dot(xcat, w1_ref[kv, half])
                if half == 1:
                    part = pltpu.roll(part, N_CMP_PAD - 1, 0)
                pre = part if pre is None else pre + part
            out = out + _dot(_silu(pre).astype(BF16), w2_ref[kv * NSA_KV_HEADS + g])
    o_ref[...] = out.astype(o_ref.dtype)


def _compress(kvc, pe2, w1d, w2p, b, s):
    return pl.pallas_call(
        _compress_body,
        grid=(b,),
        in_specs=[
            pl.BlockSpec((s, LANES), lambda r: (r, 0)),
            pl.BlockSpec((s, LANES), lambda r: (r, 1)),
            pl.BlockSpec(pe2.shape, lambda r: (0, 0, 0)),
            pl.BlockSpec(w1d.shape, lambda r: (0, 0, 0, 0)),
            pl.BlockSpec(w2p.shape, lambda r: (0, 0, 0)),
        ],
        out_specs=pl.BlockSpec((None, N_CMP_PAD, 4 * HEAD_DIM), lambda r: (r, 0, 0)),
        out_shape=jax.ShapeDtypeStruct((b, N_CMP_PAD, 4 * HEAD_DIM), BF16),
        compiler_params=_params("parallel"),
        name="compress",
    )(kvc, kvc, pe2, w1d, w2p)


ATT_T = 256
N_SEL_PAD = LANES
SEL_SHIFT = SEL_BLOCK.bit_length() - 1


def _att_consts(s):
    n_cmp = (s - CMP_BLOCK) // CMP_STRIDE + 1
    n_sel = s // SEL_BLOCK
    assert n_cmp <= N_CMP_PAD and n_sel <= N_SEL_PAD
    cs = np.arange(n_cmp)[:, None] * CMP_STRIDE
    ss = np.arange(n_sel)[None, :] * SEL_BLOCK
    ov = np.clip(np.minimum(cs + CMP_BLOCK, ss + SEL_BLOCK) - np.maximum(cs, ss), 0, None) / CMP_BLOCK
    ov_pad = np.zeros((n_sel, N_CMP_PAD), np.float32)
    ov_pad[:, :n_cmp] = ov.T
    expand = np.zeros((N_SEL_PAD, s), np.float32)
    expand[np.arange(s) // SEL_BLOCK, np.arange(s)] = 1.0
    return jnp.asarray(ov_pad, BF16), jnp.asarray(expand, BF16)


MASK_DIST = 2.0 ** 100
WIN_KEYS = WINDOW + ATT_T


def _attn_body(q_ref, kvcb_ref, ks_ref, vs_ref, kw_ref, vw_ref, gl_ref, ovt_ref, ex_ref, o_ref, out_ref,
               *, n_sel, n_variants):
    i = pl.program_id(1)
    t0 = i * ATT_T
    lane = lax.broadcasted_iota(jnp.int32, (ATT_T, LANES), 1)
    tpos = t0 + lax.broadcasted_iota(jnp.int32, (ATT_T, LANES), 0)
    gates = jax.nn.sigmoid(gl_ref[...])

    def gate(hh, branch):
        return gates[:, 3 * hh + branch:3 * hh + branch + 1]

    w0 = pl.multiple_of(jnp.maximum(t0 - WINDOW, 0), ATT_T)
    dist_w = (t0 - w0) + (lax.broadcasted_iota(jnp.int32, (ATT_T, WIN_KEYS), 0)
                          - lax.broadcasted_iota(jnp.int32, (ATT_T, WIN_KEYS), 1))
    dm_w = jnp.where((dist_w >= 0) & (dist_w < WINDOW), dist_w.astype(F32), MASK_DIST)
    k_win = kw_ref[pl.ds(w0, WIN_KEYS), :]
    v_win = vw_ref[pl.ds(w0, WIN_KEYS), :]

    d_cmp = tpos - (lane * CMP_STRIDE + (CMP_BLOCK - 1))
    valid_c = d_cmp >= 0
    dm_c = jnp.where(valid_c, d_cmp.astype(F32), MASK_DIST)
    kc = kvcb_ref[:, 0:LANES]
    vc = kvcb_ref[:, LANES:2 * LANES]

    for g in range(NSA_KV_HEADS):
        heads = [g * NSA_GROUP + h for h in range(NSA_GROUP)]
        slopes = [2.0 ** -(hh + 1) for hh in heads]
        qs = [q_ref[:, hh * LANES:(hh + 1) * LANES] * (HEAD_DIM ** -0.5) for hh in heads]

        psum = jnp.zeros((ATT_T, N_CMP_PAD), F32)
        for h, hh in enumerate(heads):
            sc = _dot_nt(qs[h], kc) - slopes[h] * dm_c
            e = jnp.exp(sc - jnp.max(sc, axis=-1, keepdims=True))
            p = e * (1.0 / jnp.sum(e, axis=-1, keepdims=True))
            p = jnp.where(valid_c, p, 0.0)
            psum = psum + p
            out_ref[hh] = gate(hh, 0) * _dot(p.astype(BF16), vc)

        p_hi = psum.astype(BF16)
        p_lo = (psum - p_hi.astype(F32)).astype(BF16)
        imp = _dot_nt(ovt_ref[...], p_hi) + _dot_nt(ovt_ref[...], p_lo)
        blk = lax.broadcasted_iota(jnp.int32, (n_sel, ATT_T), 0)
        tq = t0 + lax.broadcasted_iota(jnp.int32, (n_sel, ATT_T), 1)
        cur = tq >> SEL_SHIFT
        forced = (blk == 0) | (blk == cur) | (blk == cur - 1)
        imp = jnp.where(blk * SEL_BLOCK <= tq, imp + jnp.where(forced, FORCE_BONUS, 0.0), -1.0)
        rank = jnp.zeros((n_sel, ATT_T), F32)
        for j in range(n_sel):
            cj = imp[j:j + 1, :]
            tie = jnp.where(blk > j, 1.0, 0.0)
            rank = rank + jnp.where(cj > imp, 1.0, jnp.where(cj == imp, tie, 0.0))
        sel_t = jnp.where(rank < min(SEL_TOPK, n_sel), 1.0, 0.0)
        sel_t = jnp.concatenate([sel_t, jnp.zeros((N_SEL_PAD - n_sel, ATT_T), F32)], axis=0)
        sel = sel_t.T.astype(BF16)

        def selected(n_keys):
            chosen = _dot(sel, ex_ref[:, :n_keys])
            dist = t0 + (lax.broadcasted_iota(jnp.int32, (ATT_T, n_keys), 0)
                         - lax.broadcasted_iota(jnp.int32, (ATT_T, n_keys), 1))
            dm = jnp.where((chosen > 0.5) & (dist >= 0), dist.astype(F32), MASK_DIST)
            k_sel = ks_ref[0:n_keys, :]
            v_sel = vs_ref[0:n_keys, :]
            for h, hh in enumerate(heads):
                sc = _dot_nt(qs[h], k_sel) - slopes[h] * dm
                p = jnp.exp(sc - jnp.max(sc, axis=-1, keepdims=True))
                o_sel = _dot(p.astype(BF16), v_sel) * (1.0 / jnp.sum(p, axis=-1, keepdims=True))
                out_ref[hh] += gate(hh, 1) * o_sel

        for v in range(n_variants):
            pl.when(lax.shift_right_logical(i, 1) == v)(functools.partial(selected, 2 * (v + 1) * ATT_T))

        for h, hh in enumerate(heads):
            sc = _dot_nt(qs[h], k_win) - slopes[h] * dm_w
            p = jnp.exp(sc - jnp.max(sc, axis=-1, keepdims=True))
            o_win = _dot(p.astype(BF16), v_win) * (1.0 / jnp.sum(p, axis=-1, keepdims=True))
            out_ref[hh] += gate(hh, 2) * o_win

    for hh in range(NSA_HEADS):
        g = hh // NSA_GROUP
        o_ref[:, hh * HEAD_DIM:(hh + 1) * HEAD_DIM] = out_ref[hh][:, g * HEAD_DIM:(g + 1) * HEAD_DIM]


def _attention(q, kvcb, kvsw, gldt, ov, expand, b, s):
    nq = s // ATT_T
    n_sel = s // SEL_BLOCK
    kv_spec = lambda c: pl.BlockSpec((s, LANES), lambda r, i: (r, c))
    return pl.pallas_call(
        functools.partial(_attn_body, n_sel=n_sel, n_variants=nq // 2),
        grid=(b, nq),
        in_specs=[
            pl.BlockSpec((ATT_T, Q_AUG), lambda r, i: (r * nq + i, 0)),
            pl.BlockSpec((None, N_CMP_PAD, 4 * HEAD_DIM), lambda r, i: (r, 0, 0)),
            kv_spec(0), kv_spec(1), kv_spec(2), kv_spec(3),
            pl.BlockSpec((ATT_T, LANES), lambda r, i: (r * nq + i, 0)),
            pl.BlockSpec(ov.shape, lambda r, i: (0, 0)),
            pl.BlockSpec(expand.shape, lambda r, i: (0, 0)),
        ],
        out_specs=pl.BlockSpec((ATT_T, NSA_HEADS * HEAD_DIM), lambda r, i: (r * nq + i, 0)),
        out_shape=jax.ShapeDtypeStruct((b * s, NSA_HEADS * HEAD_DIM), F32),
        scratch_shapes=[
            pltpu.VMEM((NSA_HEADS, ATT_T, LANES), F32),
        ],
        compiler_params=_params("parallel", "parallel"),
        name="nsa_attention",
    )(q, kvcb, kvsw, kvsw, kvsw, kvsw, gldt, ov, expand)


CONV_PAD = 8
DT_LANE = 3 * NSA_HEADS


def _split_dot(a, b):
    b0 = b.astype(BF16)
    r1 = b - b0.astype(F32)
    b1 = r1.astype(BF16)
    b2 = (r1 - b1.astype(F32)).astype(BF16)
    return _dot(a, b0) + _dot(a, b1) + _dot(a, b2)


def _split_dot_rhs(a, b):
    a0 = a.astype(BF16)
    r1 = a - a0.astype(F32)
    a1 = r1.astype(BF16)
    a2 = (r1 - a1.astype(F32)).astype(BF16)
    return _dot(a0, b) + _dot(a1, b) + _dot(a2, b)


def _softplus(x):
    return jnp.maximum(x, 0.0) + jnp.log1p(jnp.exp(-jnp.abs(x)))


def _ssd_body(xbc_ref, z_ref, gldt_ref, dte_ref, cw_ref, cb_ref, dtb_e_ref, alog_e_ref, dtb_c_ref, alog_c_ref,
              d_ref, nw_ref, o_ref, state_ref, xe_ref):
    c = pl.program_id(1)
    L = SSM_CHUNK

    @pl.when(c == 0)
    def _():
        state_ref[...] = jnp.zeros(state_ref.shape, F32)
        xe_ref[0:CONV_PAD, :] = jnp.zeros((CONV_PAD, SSM_CONV_DIM), F32)

    x_raw = xbc_ref[...]
    xe_ref[CONV_PAD:CONV_PAD + L, :] = x_raw
    acc = jnp.zeros((L, SSM_CONV_DIM), F32)
    for k in range(SSM_CONV):
        lo = CONV_PAD - (SSM_CONV - 1) + k
        acc = acc + xe_ref[lo:lo + L, :] * cw_ref[k:k + 1, :]
    xe_ref[0:CONV_PAD, :] = x_raw[L - CONV_PAD:, :]
    xa = _silu(acc + cb_ref[...])
    xs = xa[:, :SSM_D_INNER]
    bm = xa[:, SSM_D_INNER:SSM_D_INNER + SSM_BC_DIM]
    cm = xa[:, SSM_D_INNER + SSM_BC_DIM:]

    row = lax.broadcasted_iota(jnp.int32, (L, L), 0)
    col = lax.broadcasted_iota(jnp.int32, (L, L), 1)
    causal = col <= row
    tri = jnp.where(causal, 1.0, 0.0).astype(BF16)
    tri_t = jnp.where(row <= col, 1.0, 0.0).astype(BF16)

    dt = _softplus(dte_ref[...] + dtb_e_ref[...])
    a = dt * (-jnp.exp(alog_e_ref[...]))
    a_cs = _split_dot(tri, a)
    a_last = a_cs[L - 1:L, :]
    dt_t = _softplus(gldt_ref[...].T[DT_LANE:DT_LANE + SSM_HEADS, :] + dtb_c_ref[...])
    a_cs_t = _split_dot_rhs(dt_t * (-jnp.exp(alog_c_ref[...])), tri_t)

    xdt = xs * dt
    xw = (xdt * jnp.exp(a_last - a_cs)).astype(BF16)
    exp_cs = jnp.exp(a_cs)
    lane = lax.broadcasted_iota(jnp.int32, (L, LANES), 1)
    rpg = SSM_HEADS // SSM_GROUPS
    gw = rpg * SSM_HEAD_DIM
    ys = []
    for g in range(SSM_GROUPS):
        bm_g = bm[:, g * SSM_STATE:(g + 1) * SSM_STATE]
        cm_g = cm[:, g * SSM_STATE:(g + 1) * SSM_STATE].astype(BF16)
        cb = _dot_nt(cm_g, bm_g.astype(BF16))
        prev = state_ref[:, g * gw:(g + 1) * gw]
        y_off = _dot(cm_g, prev.astype(BF16)) * exp_cs[:, g * gw:(g + 1) * gw]
        new_state = _dot(bm_g.T.astype(BF16), xw[:, g * gw:(g + 1) * gw])
        state_ref[:, g * gw:(g + 1) * gw] = prev * jnp.exp(a_last[:, g * gw:(g + 1) * gw]) + new_state
        for pair in range(rpg // 2):
            h0 = g * rpg + 2 * pair
            x_pair = xdt[:, h0 * SSM_HEAD_DIM:(h0 + 2) * SSM_HEAD_DIM].astype(BF16)
            halves = []
            for h in (h0, h0 + 1):
                diff = a_cs[:, h * SSM_HEAD_DIM:h * SSM_HEAD_DIM + 1] - a_cs_t[h:h + 1, :]
                decay = jnp.exp(jnp.where(causal, diff, NEG_INF))
                halves.append(_dot((cb * decay).astype(BF16), x_pair))
            ys.append(jnp.where(lane < SSM_HEAD_DIM, halves[0], halves[1])
                      + y_off[:, 2 * pair * SSM_HEAD_DIM:(2 * pair + 2) * SSM_HEAD_DIM])
    y = jnp.concatenate(ys, axis=-1) + xs * d_ref[...]
    y = y * _silu(z_ref[...])
    outs = []
    for g in range(SSM_GROUPS):
        yg = y[:, g * gw:(g + 1) * gw]
        outs.append(_rms(yg, nw_ref[:, g * gw:(g + 1) * gw]))
    o_ref[...] = jnp.concatenate(outs, axis=-1)


def _ssd(xbc, z, gldt, dte, conv_w, conv_b, dtb_e, alog_e, dtb_c, alog_c, d_e, norm_w, b, s):
    nc = s // SSM_CHUNK
    rows = lambda n: pl.BlockSpec((SSM_CHUNK, n), lambda r, c: (r * nc + c, 0))
    full = lambda arr: pl.BlockSpec(arr.shape, lambda r, c: (0,) * arr.ndim)
    consts = (conv_w, conv_b, dtb_e, alog_e, dtb_c, alog_c, d_e, norm_w)
    return pl.pallas_call(
        _ssd_body,
        grid=(b, nc),
        in_specs=[rows(SSM_CONV_DIM), rows(SSM_D_INNER), rows(LANES), rows(SSM_D_INNER)] + [full(a) for a in consts],
        out_specs=rows(SSM_D_INNER),
        out_shape=jax.ShapeDtypeStruct((b * s, SSM_D_INNER), F32),
        scratch_shapes=[
            pltpu.VMEM((SSM_STATE, SSM_D_INNER), F32),
            pltpu.VMEM((CONV_PAD + SSM_CHUNK, SSM_CONV_DIM), F32),
        ],
        compiler_params=_params("parallel", "arbitrary"),
        name="ssd",
    )(xbc, z, gldt, dte, *consts)


def _outproj_body(h_ref, oa_ref, os_ref, g_ref, w_ref, o_ref):
    na = oa_ref.shape[-1]
    y = _dot(oa_ref[...].astype(BF16), w_ref[:na, :]) + _dot(os_ref[...].astype(BF16), w_ref[na:, :])
    o_ref[...] = h_ref[...] + _rms(y, g_ref[1:2, :])


def _outproj(h, oa, os_, gains, w, layer):
    t = h.shape[0]
    rows = lambda n: pl.BlockSpec((PROJ_TM, n), lambda r: (r, 0))
    return pl.pallas_call(
        _outproj_body,
        grid=(t // PROJ_TM,),
        in_specs=[
            rows(D_MODEL), rows(oa.shape[1]), rows(os_.shape[1]),
            pl.BlockSpec((None, 4, D_MODEL), lambda r: (layer, 0, 0)),
            pl.BlockSpec(w.shape, lambda r: (0, 0)),
        ],
        out_specs=rows(D_MODEL),
        out_shape=jax.ShapeDtypeStruct((t, D_MODEL), F32),
        compiler_params=_params("parallel"),
        name="outproj",
    )(h, oa, os_, gains, w)


def _even_mixer(h, gains, layer, w_in, w_out, pe, w1, w2, conv_w, conv_b, dt_bias, a_log, d_skip, norm_w, b, s):
    q, kvc, kvsw, z, xbc, gldt, dte = _inproj(h, gains, _prep_w_in(w_in), layer)
    pe2, w1d, w2p = _prep_cmp(pe, w1, w2)
    kvcb = _compress(kvc, pe2, w1d, w2p, b, s)
    ov, expand = _att_consts(s)
    o_attn = _attention(q, kvcb, kvsw, gldt, ov, expand, b, s)
    rep = lambda v: jnp.repeat(v, SSM_HEAD_DIM)[None, :]
    o_ssm = _ssd(xbc, z, gldt, dte, conv_w, conv_b[None, :], rep(dt_bias), rep(a_log), dt_bias[:, None],
                 a_log[:, None], rep(d_skip), norm_w[None, :], b, s)
    return _outproj(h, o_attn, o_ssm, gains, w_out.astype(BF16), layer)


def kernel(x, norm_gains, ffn_w_gate, ffn_w_up, ffn_w_down, ev_w_in, ev_w_out, nsa_cmp_pe, nsa_cmp_w1, nsa_cmp_w2,
           ssm_conv_w, ssm_conv_b, ssm_dt_bias, ssm_a_log, ssm_d, ssm_norm_w, od_w_in, od_ln_w, od_ln_b, od_w_s,
           od_b_s, od_w_out):
    b, s, d = x.shape
    depth = norm_gains.shape[0]
    h = x.reshape(b * s, d)
    wg = ffn_w_gate.astype(BF16)
    wu = ffn_w_up.astype(BF16)
    wd = ffn_w_down.astype(BF16)
    od_in = od_w_in.astype(BF16)
    od_out = od_w_out.astype(BF16)
    od_lnw = od_ln_w[:, None, :]
    od_lnb = od_ln_b[:, None, :]
    od_bst = jnp.swapaxes(od_b_s, 1, 2)
    for layer in range(depth):
        i = layer // 2
        if layer % 2 == 0:
            h = _even_mixer(h, norm_gains, layer, ev_w_in[i], ev_w_out[i], nsa_cmp_pe[i], nsa_cmp_w1[i],
                            nsa_cmp_w2[i], ssm_conv_w[i], ssm_conv_b[i], ssm_dt_bias[i], ssm_a_log[i], ssm_d[i],
                            ssm_norm_w[i], b, s)
        else:
            h = _gmlp(h, norm_gains, od_in, od_lnw, od_lnb, od_w_s, od_bst, od_out, layer, i)
        h = _ffn(h, norm_gains, wg, wu, wd, layer)
    return h.reshape(b, s, d)
```

```python
import functools

import numpy as np
import jax
import jax.numpy as jnp
from jax import lax
from jax.experimental import pallas as pl
from jax.experimental.pallas import tpu as pltpu

F32 = jnp.float32
BF16 = jnp.bfloat16

D_MODEL = 1024
RMS_EPS = 1e-6
LN_EPS = 1e-5
NEG_INF = -1e30

NSA_HEADS = 8
NSA_KV_HEADS = 2
NSA_GROUP = NSA_HEADS // NSA_KV_HEADS
HEAD_DIM = 64
CMP_BLOCK = 32
CMP_STRIDE = 16
CMP_HIDDEN = 256
SEL_BLOCK = 64
SEL_TOPK = 16
WINDOW = 512
FORCE_BONUS = 1e4

SSM_HEADS = 8
SSM_HEAD_DIM = 64
SSM_D_INNER = SSM_HEADS * SSM_HEAD_DIM
SSM_GROUPS = 2
SSM_STATE = 128
SSM_CONV = 4
SSM_CHUNK = 128
SSM_BC_DIM = SSM_GROUPS * SSM_STATE
SSM_CONV_DIM = SSM_D_INNER + 2 * SSM_BC_DIM

GMLP_WIDTH = 2 * D_MODEL
GMLP_GROUPS = 8
GMLP_GROUP_DIM = GMLP_WIDTH // GMLP_GROUPS
GMLP_CHUNK = 128

LANES = 128
VMEM_LIMIT = 56 * 1024 * 1024


def _params(*sem):
    return pltpu.CompilerParams(dimension_semantics=sem, vmem_limit_bytes=VMEM_LIMIT)


def _rms(x, g):
    return x * lax.rsqrt(jnp.mean(x * x, axis=-1, keepdims=True) + RMS_EPS) * g


def _silu(x):
    return x * jax.nn.sigmoid(x)


def _dot(a, b):
    return jnp.dot(a, b, preferred_element_type=F32)


def _dot_nt(a, b):
    return lax.dot_general(a, b, (((1,), (1,)), ((), ())), preferred_element_type=F32)


def _dot_tn(a, b):
    return lax.dot_general(a, b, (((0,), (0,)), ((), ())), preferred_element_type=F32)


FFN_TM = 512
FFN_CHUNK = 256


def _ffn_body(h_ref, g_ref, wg_ref, wu_ref, wd_ref, o_ref):
    x = h_ref[...]
    xn = _rms(x, g_ref[2:3, :]).astype(BF16)
    acc = jnp.zeros(x.shape, F32)
    for c in range(wg_ref.shape[1] // FFN_CHUNK):
        cols = slice(c * FFN_CHUNK, (c + 1) * FFN_CHUNK)
        a = _dot(xn, wg_ref[:, cols])
        b = _dot(xn, wu_ref[:, cols])
        acc = acc + _dot((_silu(a) * b).astype(BF16), wd_ref[cols, :])
    o_ref[...] = x + _rms(acc, g_ref[3:4, :])


def _resident(block_shape, index_map):
    return pl.BlockSpec(block_shape, index_map, pipeline_mode=pl.Buffered(1))


def _ffn(h, gains, wg, wu, wd, layer):
    t = h.shape[0]
    hidden = wg.shape[2]
    return pl.pallas_call(
        _ffn_body,
        grid=(t // FFN_TM,),
        in_specs=[
            pl.BlockSpec((FFN_TM, D_MODEL), lambda i: (i, 0)),
            pl.BlockSpec((None, 4, D_MODEL), lambda i: (layer, 0, 0)),
            _resident((None, D_MODEL, hidden), lambda i: (layer, 0, 0)),
            _resident((None, D_MODEL, hidden), lambda i: (layer, 0, 0)),
            _resident((None, hidden, D_MODEL), lambda i: (layer, 0, 0)),
        ],
        out_specs=pl.BlockSpec((FFN_TM, D_MODEL), lambda i: (i, 0)),
        out_shape=jax.ShapeDtypeStruct((t, D_MODEL), F32),
        compiler_params=_params("parallel"),
        name="ffn",
    )(h, gains, wg, wu, wd)


GMLP_TM = 256


def _gelu(x):
    return 0.5 * x * (1.0 + lax.erf(x * np.float32(1.0 / np.sqrt(2.0))))


def _gmlp_body(h_ref, g_ref, win_ref, lnw_ref, lnb_ref, ws_ref, bs_ref, wout_ref, o_ref, gate_ref):
    xn = _rms(h_ref[...], g_ref[0:1, :]).astype(BF16)
    uv = _gelu(_dot(xn, win_ref[...]))
    u = uv[:, :GMLP_WIDTH]
    v = uv[:, GMLP_WIDTH:]
    mu = jnp.mean(v, axis=-1, keepdims=True)
    vc = v - mu
    var = jnp.mean(vc * vc, axis=-1, keepdims=True)
    v = (vc * lax.rsqrt(var + LN_EPS) * lnw_ref[...] + lnb_ref[...]).astype(BF16)
    row = lax.broadcasted_iota(jnp.int32, (GMLP_CHUNK, GMLP_CHUNK), 0)
    col = lax.broadcasted_iota(jnp.int32, (GMLP_CHUNK, GMLP_CHUNK), 1)
    causal = col <= row
    for g in range(GMLP_GROUPS):
        ws = jnp.where(causal, ws_ref[g], 0.0).astype(BF16)
        bias = bs_ref[:, g:g + 1]
        cols = slice(g * GMLP_GROUP_DIM, (g + 1) * GMLP_GROUP_DIM)
        for c in range(GMLP_TM // GMLP_CHUNK):
            rows = slice(c * GMLP_CHUNK, (c + 1) * GMLP_CHUNK)
            mixed = _dot(ws, v[rows, cols]) + bias
            gate_ref[rows, cols] = (u[rows, cols] * mixed).astype(BF16)
    y = _dot(gate_ref[...], wout_ref[...])
    o_ref[...] = h_ref[...] + _rms(y, g_ref[1:2, :])


def _gmlp(h, gains, w_in, ln_w, ln_b, w_s, b_s_t, w_out, layer, i):
    t = h.shape[0]
    const2 = lambda r: (i, 0, 0)
    return pl.pallas_call(
        _gmlp_body,
        grid=(t // GMLP_TM,),
        in_specs=[
            pl.BlockSpec((GMLP_TM, D_MODEL), lambda r: (r, 0)),
            pl.BlockSpec((None, 4, D_MODEL), lambda r: (layer, 0, 0)),
            _resident((None, D_MODEL, 2 * GMLP_WIDTH), const2),
            pl.BlockSpec((None, 1, GMLP_WIDTH), const2),
            pl.BlockSpec((None, 1, GMLP_WIDTH), const2),
            pl.BlockSpec((None, GMLP_GROUPS, GMLP_CHUNK, GMLP_CHUNK), lambda r: (i, 0, 0, 0)),
            pl.BlockSpec((None, GMLP_CHUNK, GMLP_GROUPS), const2),
            _resident((None, GMLP_WIDTH, D_MODEL), const2),
        ],
        out_specs=pl.BlockSpec((GMLP_TM, D_MODEL), lambda r: (r, 0)),
        out_shape=jax.ShapeDtypeStruct((t, D_MODEL), F32),
        scratch_shapes=[pltpu.VMEM((GMLP_TM, GMLP_WIDTH), BF16)],
        compiler_params=_params("parallel"),
        name="gmlp",
    )(h, gains, w_in, ln_w, ln_b, w_s, b_s_t, w_out)


PROJ_TM = 512
Q_PAD = NSA_HEADS * LANES
KV_PAD = NSA_KV_HEADS * LANES
BLOCK_LANE = HEAD_DIM
ALIBI_LANE = 96
ONES_LANE = HEAD_DIM
PROJ_SEGS = ((Q_PAD, BF16, False),
             (2 * LANES, F32, False),
             (KV_PAD, BF16, True),
             (KV_PAD, BF16, True),
             (KV_PAD, BF16, True),
             (KV_PAD, BF16, True),
             (SSM_D_INNER, F32, False),
             (SSM_CONV_DIM, F32, False),
             (LANES, F32, False),
             (SSM_D_INNER, F32, False))
PROJ_WIDTH = sum(seg[0] for seg in PROJ_SEGS)
FEAT_WIDTH = sum(seg[0] for seg in PROJ_SEGS if seg[2])


def _prep_w_in(w):
    nq = NSA_HEADS * HEAD_DIM
    nkv = NSA_KV_HEADS * HEAD_DIM

    def pad_heads(a, n):
        a = a.reshape(D_MODEL, n, HEAD_DIM)
        return jnp.pad(a, ((0, 0), (0, 0), (0, LANES - HEAD_DIM))).reshape(D_MODEL, n * LANES)

    o = nq
    kv = [w[:, o + j * nkv:o + (j + 1) * nkv] for j in range(6)]
    o += 6 * nkv
    gl = w[:, o:o + 3 * NSA_HEADS]
    o += 3 * NSA_HEADS
    z = w[:, o:o + SSM_D_INNER]
    o += SSM_D_INNER
    xbc = w[:, o:o + SSM_CONV_DIM]
    o += SSM_CONV_DIM
    dt = w[:, o:o + SSM_HEADS]
    gldt = jnp.concatenate([gl, dt, jnp.zeros((D_MODEL, LANES - 3 * NSA_HEADS - SSM_HEADS), w.dtype)], axis=1)
    dte = jnp.repeat(dt, SSM_HEAD_DIM, axis=1)
    out = jnp.concatenate([pad_heads(w[:, :nq], NSA_HEADS), kv[0], kv[1]]
                          + [pad_heads(a, NSA_KV_HEADS) for a in kv[2:]] + [z, xbc, gldt, dte], axis=1)
    assert out.shape[1] == PROJ_WIDTH
    return out.astype(BF16)


def _position_features(s):
    n_sel = s // SEL_BLOCK
    assert BLOCK_LANE + n_sel <= ALIBI_LANE and ALIBI_LANE + 2 <= LANES
    pos = np.arange(s)
    alibi = np.zeros((s, LANES), np.float32)
    alibi[:, ALIBI_LANE] = pos // SEL_BLOCK
    alibi[:, ALIBI_LANE + 1] = pos % SEL_BLOCK
    k_sel = alibi.copy()
    k_sel[pos, BLOCK_LANE + pos // SEL_BLOCK] = 1.0
    ones = np.zeros((s, LANES), np.float32)
    ones[:, ONES_LANE] = 1.0
    feats = [np.tile(a, (1, NSA_KV_HEADS)) for a in (k_sel, ones, alibi, ones)]
    return jnp.asarray(np.concatenate(feats, axis=1), BF16)


def _inproj_body(h_ref, g_ref, w_ref, feat_ref, *out_refs):
    xn = _rms(h_ref[...], g_ref[0:1, :]).astype(BF16)
    off = 0
    feat_off = 0
    for ref, (n, dtype, has_feat) in zip(out_refs, PROJ_SEGS):
        val = _dot(xn, w_ref[:, off:off + n]).astype(dtype)
        if has_feat:
            val = val + feat_ref[:, feat_off:feat_off + n]
            feat_off += n
        ref[...] = val
        off += n


def _inproj(h, gains, w, feats, layer):
    t = h.shape[0]
    tiles_per_seq = feats.shape[0] // PROJ_TM
    return pl.pallas_call(
        _inproj_body,
        grid=(t // PROJ_TM,),
        in_specs=[
            pl.BlockSpec((PROJ_TM, D_MODEL), lambda r: (r, 0)),
            pl.BlockSpec((None, 4, D_MODEL), lambda r: (layer, 0, 0)),
            _resident((D_MODEL, PROJ_WIDTH), lambda r: (0, 0)),
            pl.BlockSpec((PROJ_TM, FEAT_WIDTH), lambda r: (lax.rem(r, tiles_per_seq), 0)),
        ],
        out_specs=[pl.BlockSpec((PROJ_TM, seg[0]), lambda r: (r, 0)) for seg in PROJ_SEGS],
        out_shape=[jax.ShapeDtypeStruct((t, seg[0]), seg[1]) for seg in PROJ_SEGS],
        compiler_params=_params("parallel"),
        name="inproj",
    )(h, gains, w, feats)


N_CMP_PAD = 128
CMP_HALF = CMP_BLOCK // CMP_STRIDE
CMP_SLOTS = 2 * NSA_KV_HEADS


def _prep_cmp(pe, w1, w2):
    pe2 = jnp.concatenate([pe, pe], axis=-1)
    w1r = w1.reshape(2, CMP_HALF, CMP_STRIDE, HEAD_DIM, CMP_HIDDEN)
    w1d = jnp.concatenate([w1r, w1r], axis=3).reshape(2, CMP_HALF, CMP_STRIDE * LANES, CMP_HIDDEN)
    slots = []
    for kv in range(2):
        for g in range(NSA_KV_HEADS):
            slot = kv * NSA_KV_HEADS + g
            slots.append(jnp.pad(w2[kv], ((0, 0), (slot * LANES, (CMP_SLOTS - slot) * LANES - HEAD_DIM))))
    return pe2, w1d.astype(BF16), jnp.stack(slots).astype(BF16)


def _compress_body(kc_ref, vc_ref, pe_ref, w1_ref, w2_ref, o_ref):
    lane = lax.broadcasted_iota(jnp.int32, (N_CMP_PAD, LANES), 1)
    out = jnp.zeros((N_CMP_PAD, CMP_SLOTS * LANES), F32)
    for kv in range(2):
        src = (kc_ref, vc_ref)[kv]
        pieces = [src[pl.ds(l, N_CMP_PAD, stride=CMP_STRIDE), :] for l in range(CMP_STRIDE)]
        for g in range(NSA_KV_HEADS):
            in_group = (lane >= g * HEAD_DIM) & (lane < (g + 1) * HEAD_DIM)
            pre = None
            for half in range(CMP_HALF):
                xs = [jnp.where(in_group, pieces[l] + pe_ref[kv, half * CMP_STRIDE + l:half * CMP_STRIDE + l + 1, :], 0.0)
                      for l in range(CMP_STRIDE)]
                xcat = jnp.concatenate(xs, axis=-1).astype(BF16)
                part = _dot(xcat, w1_ref[kv, half])
                if half == 1:
                    part = pltpu.roll(part, N_CMP_PAD - 1, 0)
                pre = part if pre is None else pre + part
            out = out + _dot(_silu(pre).astype(BF16), w2_ref[kv * NSA_KV_HEADS + g])
    o_ref[...] = out.astype(o_ref.dtype)


def _compress(kvc, pe2, w1d, w2p, b, s):
    return pl.pallas_call(
        _compress_body,
        grid=(b,),
        in_specs=[
            pl.BlockSpec((s, LANES), lambda r: (r, 0)),
            pl.BlockSpec((s, LANES), lambda r: (r, 1)),
            pl.BlockSpec(pe2.shape, lambda r: (0, 0, 0)),
            pl.BlockSpec(w1d.shape, lambda r: (0, 0, 0, 0)),
            pl.BlockSpec(w2p.shape, lambda r: (0, 0, 0)),
        ],
        out_specs=pl.BlockSpec((None, N_CMP_PAD, CMP_SLOTS * LANES), lambda r: (r, 0, 0)),
        out_shape=jax.ShapeDtypeStruct((b, N_CMP_PAD, CMP_SLOTS * LANES), BF16),
        compiler_params=_params("parallel"),
        name="compress",
    )(kvc, kvc, pe2, w1d, w2p)


ATT_T = 256
SEL_SHIFT = SEL_BLOCK.bit_length() - 1
MASK_DIST = 2.0 ** 100
MASK_BIAS = -(2.0 ** 100)
WIN_KEYS = WINDOW + ATT_T
DIAG_KEYS = 2 * ATT_T


def _overlap_t(s):
    n_cmp = (s - CMP_BLOCK) // CMP_STRIDE + 1
    n_sel = s // SEL_BLOCK
    assert n_cmp <= N_CMP_PAD
    cs = np.arange(n_cmp)[None, :] * CMP_STRIDE
    ss = np.arange(n_sel)[:, None] * SEL_BLOCK
    ov = np.clip(np.minimum(cs + CMP_BLOCK, ss + SEL_BLOCK) - np.maximum(cs, ss), 0, None) / CMP_BLOCK
    return jnp.asarray(np.pad(ov, ((0, 0), (0, N_CMP_PAD - n_cmp))), BF16)


def _softmax_pv(sc, v):
    p = jnp.exp(sc - jnp.max(sc, axis=-1, keepdims=True))
    o = _dot(p.astype(BF16), v)
    return o * (1.0 / o[:, ONES_LANE:ONES_LANE + 1])


def _attn_body(q_ref, kvcb_ref, ks_ref, vs_ref, kw_ref, vw_ref, gl_ref, ovt_ref, o_ref, out_ref,
               *, n_sel, n_variants):
    i = pl.program_id(1)
    t0 = i * ATT_T
    lane = lax.broadcasted_iota(jnp.int32, (ATT_T, LANES), 1)
    tpos = t0 + lax.broadcasted_iota(jnp.int32, (ATT_T, LANES), 0)
    lane_row = lax.broadcasted_iota(jnp.int32, (1, LANES), 1)
    gates = jax.nn.sigmoid(gl_ref[...])

    def gate(hh, branch):
        return gates[:, 3 * hh + branch:3 * hh + branch + 1]

    def rel(n):
        return lax.broadcasted_iota(jnp.int32, (ATT_T, n), 0) - lax.broadcasted_iota(jnp.int32, (ATT_T, n), 1)

    w0 = pl.multiple_of(jnp.maximum(t0 - WINDOW, 0), ATT_T)
    dist_w = (t0 - w0) + rel(WIN_KEYS)
    bias_w = jnp.where((dist_w >= 0) & (dist_w < WINDOW), 0.0, MASK_BIAS)

    d_cmp = tpos - (lane * CMP_STRIDE + (CMP_BLOCK - 1))
    valid_c = d_cmp >= 0
    dm_c = jnp.where(valid_c, d_cmp.astype(F32), MASK_DIST)

    for g in range(NSA_KV_HEADS):
        heads = [g * NSA_GROUP + h for h in range(NSA_GROUP)]
        slopes = [2.0 ** -(hh + 1) for hh in heads]
        group = slice(g * LANES, (g + 1) * LANES)
        qs = [q_ref[:, hh * LANES:(hh + 1) * LANES] * (HEAD_DIM ** -0.5) for hh in heads]
        q_alibi = [qs[h] + (jnp.where(lane_row == ALIBI_LANE, slopes[h] * SEL_BLOCK, 0.0)
                            + jnp.where(lane_row == ALIBI_LANE + 1, slopes[h], 0.0)).astype(BF16)
                   for h in range(NSA_GROUP)]

        kc = kvcb_ref[:, group]
        vc = kvcb_ref[:, NSA_KV_HEADS * LANES + g * LANES:NSA_KV_HEADS * LANES + (g + 1) * LANES]
        psum = jnp.zeros((ATT_T, N_CMP_PAD), F32)
        for h, hh in enumerate(heads):
            sc = _dot_nt(qs[h], kc) - slopes[h] * dm_c
            e = jnp.exp(sc - jnp.max(sc, axis=-1, keepdims=True))
            p = e * (1.0 / jnp.sum(e, axis=-1, keepdims=True))
            p = jnp.where(valid_c, p, 0.0)
            psum = psum + p
            out_ref[hh] = gate(hh, 0) * _dot(p.astype(BF16), vc)

        p_hi = psum.astype(BF16)
        p_lo = (psum - p_hi.astype(F32)).astype(BF16)
        imp = _dot_nt(ovt_ref[...], p_hi) + _dot_nt(ovt_ref[...], p_lo)
        blk = lax.broadcasted_iota(jnp.int32, (n_sel, ATT_T), 0)
        tq = t0 + lax.broadcasted_iota(jnp.int32, (n_sel, ATT_T), 1)
        cur = tq >> SEL_SHIFT
        forced = (blk == 0) | (blk == cur) | (blk == cur - 1)
        imp = jnp.where(blk * SEL_BLOCK <= tq, imp + jnp.where(forced, FORCE_BONUS, 0.0), -1.0)
        rank = jnp.zeros((n_sel, ATT_T), F32)
        for j in range(n_sel):
            cj = imp[j:j + 1, :]
            tie = jnp.where(blk > j, 1.0, 0.0)
            rank = rank + jnp.where(cj > imp, 1.0, jnp.where(cj == imp, tie, 0.0))
        drop_t = jnp.where(rank < min(SEL_TOPK, n_sel), 0.0, MASK_BIAS)
        drop_t = jnp.concatenate([jnp.zeros((BLOCK_LANE, ATT_T), F32), drop_t,
                                  jnp.zeros((LANES - BLOCK_LANE - n_sel, ATT_T), F32)], axis=0)
        drop = drop_t.T.astype(BF16)
        q_sel = [q_alibi[h] + drop for h in range(NSA_GROUP)]

        def selected(n_keys):
            k_sel = ks_ref[0:n_keys, group]
            v_sel = vs_ref[0:n_keys, group]
            causal = jnp.where(rel(DIAG_KEYS) + (t0 - (n_keys - DIAG_KEYS)) >= 0, 0.0, MASK_BIAS)
            for h, hh in enumerate(heads):
                sc = _dot_nt(q_sel[h], k_sel)
                tail = sc[:, n_keys - DIAG_KEYS:] + causal
                sc = tail if n_keys == DIAG_KEYS else jnp.concatenate([sc[:, :n_keys - DIAG_KEYS], tail], axis=1)
                out_ref[hh] += gate(hh, 1) * _softmax_pv(sc, v_sel)

        for v in range(n_variants):
            pl.when(lax.shift_right_logical(i, 1) == v)(functools.partial(selected, (v + 1) * DIAG_KEYS))

        k_win = kw_ref[pl.ds(w0, WIN_KEYS), group]
        v_win = vw_ref[pl.ds(w0, WIN_KEYS), group]
        for h, hh in enumerate(heads):
            out_ref[hh] += gate(hh, 2) * _softmax_pv(_dot_nt(q_alibi[h], k_win) + bias_w, v_win)

    for hh in range(NSA_HEADS):
        o_ref[:, hh * HEAD_DIM:(hh + 1) * HEAD_DIM] = out_ref[hh][:, 0:HEAD_DIM]


def _attention(q, kvcb, ks, vs, kw, vw, gldt, ovt, b, s):
    nq = s // ATT_T
    n_sel = s // SEL_BLOCK
    assert nq % 2 == 0
    seq = pl.BlockSpec((s, KV_PAD), lambda r, i: (r, 0))
    return pl.pallas_call(
        functools.partial(_attn_body, n_sel=n_sel, n_variants=nq // 2),
        grid=(b, nq),
        in_specs=[
            pl.BlockSpec((ATT_T, Q_PAD), lambda r, i: (r * nq + i, 0)),
            pl.BlockSpec((None, N_CMP_PAD, CMP_SLOTS * LANES), lambda r, i: (r, 0, 0)),
            seq, seq, seq, seq,
            pl.BlockSpec((ATT_T, LANES), lambda r, i: (r * nq + i, 0)),
            pl.BlockSpec(ovt.shape, lambda r, i: (0, 0)),
        ],
        out_specs=pl.BlockSpec((ATT_T, NSA_HEADS * HEAD_DIM), lambda r, i: (r * nq + i, 0)),
        out_shape=jax.ShapeDtypeStruct((b * s, NSA_HEADS * HEAD_DIM), F32),
        scratch_shapes=[pltpu.VMEM((NSA_HEADS, ATT_T, LANES), F32)],
        compiler_params=_params("parallel", "parallel"),
        name="nsa_attention",
    )(q, kvcb, ks, vs, kw, vw, gldt, ovt)


CONV_PAD = 8
DT_LANE = 3 * NSA_HEADS


def _split_dot(a, b):
    b0 = b.astype(BF16)
    r1 = b - b0.astype(F32)
    b1 = r1.astype(BF16)
    b2 = (r1 - b1.astype(F32)).astype(BF16)
    return _dot(a, b0) + _dot(a, b1) + _dot(a, b2)


def _split_dot_rhs(a, b):
    a0 = a.astype(BF16)
    r1 = a - a0.astype(F32)
    a1 = r1.astype(BF16)
    a2 = (r1 - a1.astype(F32)).astype(BF16)
    return _dot(a0, b) + _dot(a1, b) + _dot(a2, b)


def _softplus(x):
    return jnp.maximum(x, 0.0) + jnp.log1p(jnp.exp(-jnp.abs(x)))


def _ssd_body(xbc_ref, z_ref, gldt_ref, dte_ref, cw_ref, cb_ref, dtb_e_ref, alog_e_ref, dtb_c_ref, alog_c_ref,
              d_ref, nw_ref, o_ref, state_ref, xe_ref):
    c = pl.program_id(1)
    L = SSM_CHUNK

    @pl.when(c == 0)
    def _():
        state_ref[...] = jnp.zeros(state_ref.shape, F32)
        xe_ref[0:CONV_PAD, :] = jnp.zeros((CONV_PAD, SSM_CONV_DIM), F32)

    x_raw = xbc_ref[...]
    xe_ref[CONV_PAD:CONV_PAD + L, :] = x_raw
    acc = jnp.zeros((L, SSM_CONV_DIM), F32)
    for k in range(SSM_CONV):
        lo = CONV_PAD - (SSM_CONV - 1) + k
        acc = acc + xe_ref[lo:lo + L, :] * cw_ref[k:k + 1, :]
    xe_ref[0:CONV_PAD, :] = x_raw[L - CONV_PAD:, :]
    xa = _silu(acc + cb_ref[...])
    xs = xa[:, :SSM_D_INNER]
    bm = xa[:, SSM_D_INNER:SSM_D_INNER + SSM_BC_DIM]
    cm = xa[:, SSM_D_INNER + SSM_BC_DIM:]

    row = lax.broadcasted_iota(jnp.int32, (L, L), 0)
    col = lax.broadcasted_iota(jnp.int32, (L, L), 1)
    causal = col <= row
    tri = jnp.where(causal, 1.0, 0.0).astype(BF16)
    tri_t = jnp.where(row <= col, 1.0, 0.0).astype(BF16)

    dt = _softplus(dte_ref[...] + dtb_e_ref[...])
    a = dt * (-jnp.exp(alog_e_ref[...]))
    a_cs = _split_dot(tri, a)
    a_last = a_cs[L - 1:L, :]
    dt_t = _softplus(gldt_ref[...].T[DT_LANE:DT_LANE + SSM_HEADS, :] + dtb_c_ref[...])
    a_cs_t = _split_dot_rhs(dt_t * (-jnp.exp(alog_c_ref[...])), tri_t)

    xdt = xs * dt
    xw = (xdt * jnp.exp(a_last - a_cs)).astype(BF16)
    exp_cs = jnp.exp(a_cs)
    lane = lax.broadcasted_iota(jnp.int32, (L, LANES), 1)
    rpg = SSM_HEADS // SSM_GROUPS
    gw = rpg * SSM_HEAD_DIM
    ys = []
    for g in range(SSM_GROUPS):
        bm_g = bm[:, g * SSM_STATE:(g + 1) * SSM_STATE]
        cm_g = cm[:, g * SSM_STATE:(g + 1) * SSM_STATE].astype(BF16)
        cb = _dot_nt(cm_g, bm_g.astype(BF16))
        prev = state_ref[:, g * gw:(g + 1) * gw]
        y_off = _dot(cm_g, prev.astype(BF16)) * exp_cs[:, g * gw:(g + 1) * gw]
        new_state = _dot(bm_g.T.astype(BF16), xw[:, g * gw:(g + 1) * gw])
        state_ref[:, g * gw:(g + 1) * gw] = prev * jnp.exp(a_last[:, g * gw:(g + 1) * gw]) + new_state
        for pair in range(rpg // 2):
            h0 = g * rpg + 2 * pair
            x_pair = xdt[:, h0 * SSM_HEAD_DIM:(h0 + 2) * SSM_HEAD_DIM].astype(BF16)
            halves = []
            for h in (h0, h0 + 1):
                diff = a_cs[:, h * SSM_HEAD_DIM:h * SSM_HEAD_DIM + 1] - a_cs_t[h:h + 1, :]
                decay = jnp.exp(jnp.where(causal, diff, NEG_INF))
                halves.append(_dot((cb * decay).astype(BF16), x_pair))
            ys.append(jnp.where(lane < SSM_HEAD_DIM, halves[0], halves[1])
                      + y_off[:, 2 * pair * SSM_HEAD_DIM:(2 * pair + 2) * SSM_HEAD_DIM])
    y = jnp.concatenate(ys, axis=-1) + xs * d_ref[...]
    y = y * _silu(z_ref[...])
    outs = []
    for g in range(SSM_GROUPS):
        yg = y[:, g * gw:(g + 1) * gw]
        outs.append(_rms(yg, nw_ref[:, g * gw:(g + 1) * gw]))
    o_ref[...] = jnp.concatenate(outs, axis=-1)


def _ssd(xbc, z, gldt, dte, conv_w, conv_b, dtb_e, alog_e, dtb_c, alog_c, d_e, norm_w, b, s):
    nc = s // SSM_CHUNK
    rows = lambda n: pl.BlockSpec((SSM_CHUNK, n), lambda r, c: (r * nc + c, 0))
    full = lambda arr: pl.BlockSpec(arr.shape, lambda r, c: (0,) * arr.ndim)
    consts = (conv_w, conv_b, dtb_e, alog_e, dtb_c, alog_c, d_e, norm_w)
    return pl.pallas_call(
        _ssd_body,
        grid=(b, nc),
        in_specs=[rows(SSM_CONV_DIM), rows(SSM_D_INNER), rows(LANES), rows(SSM_D_INNER)] + [full(a) for a in consts],
        out_specs=rows(SSM_D_INNER),
        out_shape=jax.ShapeDtypeStruct((b * s, SSM_D_INNER), F32),
        scratch_shapes=[
            pltpu.VMEM((SSM_STATE, SSM_D_INNER), F32),
            pltpu.VMEM((CONV_PAD + SSM_CHUNK, SSM_CONV_DIM), F32),
        ],
        compiler_params=_params("parallel", "arbitrary"),
        name="ssd",
    )(xbc, z, gldt, dte, *consts)


def _outproj_body(h_ref, oa_ref, os_ref, g_ref, w_ref, o_ref):
    na = oa_ref.shape[-1]
    y = _dot(oa_ref[...].astype(BF16), w_ref[:na, :]) + _dot(os_ref[...].astype(BF16), w_ref[na:, :])
    o_ref[...] = h_ref[...] + _rms(y, g_ref[1:2, :])


def _outproj(h, oa, os_, gains, w, layer):
    t = h.shape[0]
    rows = lambda n: pl.BlockSpec((PROJ_TM, n), lambda r: (r, 0))
    return pl.pallas_call(
        _outproj_body,
        grid=(t // PROJ_TM,),
        in_specs=[
            rows(D_MODEL), rows(oa.shape[1]), rows(os_.shape[1]),
            pl.BlockSpec((None, 4, D_MODEL), lambda r: (layer, 0, 0)),
            pl.BlockSpec(w.shape, lambda r: (0, 0)),
        ],
        out_specs=rows(D_MODEL),
        out_shape=jax.ShapeDtypeStruct((t, D_MODEL), F32),
        compiler_params=_params("parallel"),
        name="outproj",
    )(h, oa, os_, gains, w)


def _even_mixer(h, gains, layer, w_in, w_out, pe, w1, w2, conv_w, conv_b, dt_bias, a_log, d_skip, norm_w, b, s):
    q, kvc, ks, vs, kw, vw, z, xbc, gldt, dte = _inproj(h, gains, _prep_w_in(w_in), _position_features(s), layer)
    pe2, w1d, w2p = _prep_cmp(pe, w1, w2)
    kvcb = _compress(kvc, pe2, w1d, w2p, b, s)
    o_attn = _attention(q, kvcb, ks, vs, kw, vw, gldt, _overlap_t(s), b, s)
    rep = lambda v: jnp.repeat(v, SSM_HEAD_DIM)[None, :]
    o_ssm = _ssd(xbc, z, gldt, dte, conv_w, conv_b[None, :], rep(dt_bias), rep(a_log), dt_bias[:, None],
                 a_log[:, None], rep(d_skip), norm_w[None, :], b, s)
    return _outproj(h, o_attn, o_ssm, gains, w_out.astype(BF16), layer)


def kernel(x, norm_gains, ffn_w_gate, ffn_w_up, ffn_w_down, ev_w_in, ev_w_out, nsa_cmp_pe, nsa_cmp_w1, nsa_cmp_w2,
           ssm_conv_w, ssm_conv_b, ssm_dt_bias, ssm_a_log, ssm_d, ssm_norm_w, od_w_in, od_ln_w, od_ln_b, od_w_s,
           od_b_s, od_w_out):
    b, s, d = x.shape
    depth = norm_gains.shape[0]
    h = x.reshape(b * s, d)
    wg = ffn_w_gate.astype(BF16)
    wu = ffn_w_up.astype(BF16)
    wd = ffn_w_down.astype(BF16)
    od_in = od_w_in.astype(BF16)
    od_out = od_w_out.astype(BF16)
    od_lnw = od_ln_w[:, None, :]
    od_lnb = od_ln_b[:, None, :]
    od_bst = jnp.swapaxes(od_b_s, 1, 2)
    for layer in range(depth):
        i = layer // 2
        if layer % 2 == 0:
            h = _even_mixer(h, norm_gains, layer, ev_w_in[i], ev_w_out[i], nsa_cmp_pe[i], nsa_cmp_w1[i],
                            nsa_cmp_w2[i], ssm_conv_w[i], ssm_conv_b[i], ssm_dt_bias[i], ssm_a_log[i], ssm_d[i],
                            ssm_norm_w[i], b, s)
        else:
            h = _gmlp(h, norm_gains, od_in, od_lnw, od_lnb, od_w_s, od_bst, od_out, layer, i)
        h = _ffn(h, norm_gains, wg, wu, wd, layer)
    return h.reshape(b, s, d)
```

```python
import functools

import numpy as np
import jax
import jax.numpy as jnp
from jax import lax
from jax.experimental import pallas as pl
from jax.experimental.pallas import tpu as pltpu

F32 = jnp.float32
BF16 = jnp.bfloat16

D_MODEL = 1024
RMS_EPS = 1e-6
LN_EPS = 1e-5
NEG_INF = -1e30

NSA_HEADS = 8
NSA_KV_HEADS = 2
NSA_GROUP = NSA_HEADS // NSA_KV_HEADS
HEAD_DIM = 64
CMP_BLOCK = 32
CMP_STRIDE = 16
CMP_HIDDEN = 256
SEL_BLOCK = 64
SEL_TOPK = 16
WINDOW = 512
FORCE_BONUS = 1e4

SSM_HEADS = 8
SSM_HEAD_DIM = 64
SSM_D_INNER = SSM_HEADS * SSM_HEAD_DIM
SSM_GROUPS = 2
SSM_STATE = 128
SSM_CONV = 4
SSM_CHUNK = 128
SSM_BC_DIM = SSM_GROUPS * SSM_STATE
SSM_CONV_DIM = SSM_D_INNER + 2 * SSM_BC_DIM

GMLP_WIDTH = 2 * D_MODEL
GMLP_GROUPS = 8
GMLP_GROUP_DIM = GMLP_WIDTH // GMLP_GROUPS
GMLP_CHUNK = 128

LANES = 128
VMEM_LIMIT = 56 * 1024 * 1024


def _params(*sem):
    return pltpu.CompilerParams(dimension_semantics=sem, vmem_limit_bytes=VMEM_LIMIT)


def _rms(x, g):
    return x * lax.rsqrt(jnp.mean(x * x, axis=-1, keepdims=True) + RMS_EPS) * g


def _silu(x):
    return x * jax.nn.sigmoid(x)


def _dot(a, b):
    return jnp.dot(a, b, preferred_element_type=F32)


def _dot_nt(a, b):
    return lax.dot_general(a, b, (((1,), (1,)), ((), ())), preferred_element_type=F32)


def _resident(block_shape, index_map):
    return pl.BlockSpec(block_shape, index_map, pipeline_mode=pl.Buffered(1))


FFN_TM = 512
FFN_CHUNK = 256


def _ffn_body(h_ref, g_ref, wg_ref, wu_ref, wd_ref, o_ref):
    x = h_ref[...]
    xn = _rms(x, g_ref[2:3, :]).astype(BF16)
    acc = jnp.zeros(x.shape, F32)
    for c in range(wg_ref.shape[1] // FFN_CHUNK):
        cols = slice(c * FFN_CHUNK, (c + 1) * FFN_CHUNK)
        a = _dot(xn, wg_ref[:, cols])
        b = _dot(xn, wu_ref[:, cols])
        acc = acc + _dot((_silu(a) * b).astype(BF16), wd_ref[cols, :])
    o_ref[...] = x + _rms(acc, g_ref[3:4, :])


def _ffn(h, gains, wg, wu, wd, layer):
    t = h.shape[0]
    hidden = wg.shape[2]
    return pl.pallas_call(
        _ffn_body,
        grid=(t // FFN_TM,),
        in_specs=[
            pl.BlockSpec((FFN_TM, D_MODEL), lambda i: (i, 0)),
            pl.BlockSpec((None, 4, D_MODEL), lambda i: (layer, 0, 0)),
            _resident((None, D_MODEL, hidden), lambda i: (layer, 0, 0)),
            _resident((None, D_MODEL, hidden), lambda i: (layer, 0, 0)),
            _resident((None, hidden, D_MODEL), lambda i: (layer, 0, 0)),
        ],
        out_specs=pl.BlockSpec((FFN_TM, D_MODEL), lambda i: (i, 0)),
        out_shape=jax.ShapeDtypeStruct((t, D_MODEL), F32),
        compiler_params=_params("parallel"),
        name="ffn",
    )(h, gains, wg, wu, wd)


GMLP_TM = 256


def _gelu(x):
    return 0.5 * x * (1.0 + lax.erf(x * np.float32(1.0 / np.sqrt(2.0))))


def _gmlp_body(h_ref, g_ref, win_ref, lnw_ref, lnb_ref, ws_ref, bs_ref, wout_ref, o_ref, gate_ref):
    xn = _rms(h_ref[...], g_ref[0:1, :]).astype(BF16)
    uv = _gelu(_dot(xn, win_ref[...]))
    u = uv[:, :GMLP_WIDTH]
    v = uv[:, GMLP_WIDTH:]
    mu = jnp.mean(v, axis=-1, keepdims=True)
    vc = v - mu
    var = jnp.mean(vc * vc, axis=-1, keepdims=True)
    v = (vc * lax.rsqrt(var + LN_EPS) * lnw_ref[...] + lnb_ref[...]).astype(BF16)
    row = lax.broadcasted_iota(jnp.int32, (GMLP_CHUNK, GMLP_CHUNK), 0)
    col = lax.broadcasted_iota(jnp.int32, (GMLP_CHUNK, GMLP_CHUNK), 1)
    causal = col <= row
    for g in range(GMLP_GROUPS):
        ws = jnp.where(causal, ws_ref[g], 0.0).astype(BF16)
        bias = bs_ref[:, g:g + 1]
        cols = slice(g * GMLP_GROUP_DIM, (g + 1) * GMLP_GROUP_DIM)
        for c in range(GMLP_TM // GMLP_CHUNK):
            rows = slice(c * GMLP_CHUNK, (c + 1) * GMLP_CHUNK)
            mixed = _dot(ws, v[rows, cols]) + bias
            gate_ref[rows, cols] = (u[rows, cols] * mixed).astype(BF16)
    y = _dot(gate_ref[...], wout_ref[...])
    o_ref[...] = h_ref[...] + _rms(y, g_ref[1:2, :])


def _gmlp(h, gains, w_in, ln_w, ln_b, w_s, b_s_t, w_out, layer, i):
    t = h.shape[0]
    const2 = lambda r: (i, 0, 0)
    return pl.pallas_call(
        _gmlp_body,
        grid=(t // GMLP_TM,),
        in_specs=[
            pl.BlockSpec((GMLP_TM, D_MODEL), lambda r: (r, 0)),
            pl.BlockSpec((None, 4, D_MODEL), lambda r: (layer, 0, 0)),
            _resident((None, D_MODEL, 2 * GMLP_WIDTH), const2),
            pl.BlockSpec((None, 1, GMLP_WIDTH), const2),
            pl.BlockSpec((None, 1, GMLP_WIDTH), const2),
            pl.BlockSpec((None, GMLP_GROUPS, GMLP_CHUNK, GMLP_CHUNK), lambda r: (i, 0, 0, 0)),
            pl.BlockSpec((None, GMLP_CHUNK, GMLP_GROUPS), const2),
            _resident((None, GMLP_WIDTH, D_MODEL), const2),
        ],
        out_specs=pl.BlockSpec((GMLP_TM, D_MODEL), lambda r: (r, 0)),
        out_shape=jax.ShapeDtypeStruct((t, D_MODEL), F32),
        scratch_shapes=[pltpu.VMEM((GMLP_TM, GMLP_WIDTH), BF16)],
        compiler_params=_params("parallel"),
        name="gmlp",
    )(h, gains, w_in, ln_w, ln_b, w_s, b_s_t, w_out)


PROJ_TM = 512
KV_PAD = NSA_KV_HEADS * LANES
BLOCK_LANE = HEAD_DIM
ALIBI_LANE = 96
V_ROWS = HEAD_DIM + 16
ONES_ROW = HEAD_DIM
ROW_SEGS = ((2 * LANES, F32, False),
            (KV_PAD, BF16, True),
            (KV_PAD, BF16, True),
            (SSM_D_INNER, F32, False),
            (SSM_CONV_DIM, F32, False),
            (LANES, F32, False),
            (SSM_D_INNER, F32, False))
COL_SEGS = ((NSA_HEADS * HEAD_DIM, False),
            (NSA_KV_HEADS * V_ROWS, True),
            (NSA_KV_HEADS * V_ROWS, True))
ROW_WIDTH = sum(seg[0] for seg in ROW_SEGS)
COL_WIDTH = sum(seg[0] for seg in COL_SEGS)
FEAT_WIDTH = sum(seg[0] for seg in ROW_SEGS if seg[2])


def _prep_w_in(w):
    nq = NSA_HEADS * HEAD_DIM
    nkv = NSA_KV_HEADS * HEAD_DIM

    def pad_groups(a, width):
        a = a.reshape(D_MODEL, NSA_KV_HEADS, HEAD_DIM)
        return jnp.pad(a, ((0, 0), (0, 0), (0, width - HEAD_DIM))).reshape(D_MODEL, NSA_KV_HEADS * width)

    o = nq
    kc, vc, ks, vs, kw, vw = [w[:, o + j * nkv:o + (j + 1) * nkv] for j in range(6)]
    o += 6 * nkv
    gl = w[:, o:o + 3 * NSA_HEADS]
    o += 3 * NSA_HEADS
    z = w[:, o:o + SSM_D_INNER]
    o += SSM_D_INNER
    xbc = w[:, o:o + SSM_CONV_DIM]
    o += SSM_CONV_DIM
    dt = w[:, o:o + SSM_HEADS]
    gldt = jnp.concatenate([gl, dt, jnp.zeros((D_MODEL, LANES - 3 * NSA_HEADS - SSM_HEADS), w.dtype)], axis=1)
    dte = jnp.repeat(dt, SSM_HEAD_DIM, axis=1)
    rows = jnp.concatenate([kc, vc, pad_groups(ks, LANES), pad_groups(kw, LANES), z, xbc, gldt, dte], axis=1)
    cols = jnp.concatenate([w[:, :nq], pad_groups(vs, V_ROWS), pad_groups(vw, V_ROWS)], axis=1).T
    assert rows.shape[1] == ROW_WIDTH and cols.shape[0] == COL_WIDTH
    return rows.astype(BF16), cols.astype(BF16)


def _position_features(s):
    n_sel = s // SEL_BLOCK
    assert BLOCK_LANE + n_sel <= ALIBI_LANE and ALIBI_LANE + 2 <= LANES
    pos = np.arange(s)
    alibi = np.zeros((s, LANES), np.float32)
    alibi[:, ALIBI_LANE] = pos // SEL_BLOCK
    alibi[:, ALIBI_LANE + 1] = pos % SEL_BLOCK
    k_sel = alibi.copy()
    k_sel[pos, BLOCK_LANE + pos // SEL_BLOCK] = 1.0
    feats = [np.tile(a, (1, NSA_KV_HEADS)) for a in (k_sel, alibi)]
    return jnp.asarray(np.concatenate(feats, axis=1), BF16)


def _inproj_body(h_ref, g_ref, wr_ref, wc_ref, feat_ref, *out_refs):
    xn = _rms(h_ref[...], g_ref[0:1, :]).astype(BF16)
    row_refs = out_refs[:len(ROW_SEGS)]
    col_refs = out_refs[len(ROW_SEGS):]
    off = 0
    feat_off = 0
    for ref, (n, dtype, has_feat) in zip(row_refs, ROW_SEGS):
        val = _dot(xn, wr_ref[:, off:off + n]).astype(dtype)
        if has_feat:
            val = val + feat_ref[:, feat_off:feat_off + n]
            feat_off += n
        ref[...] = val
        off += n
    off = 0
    for ref, (n, has_ones) in zip(col_refs, COL_SEGS):
        val = _dot_nt(wc_ref[off:off + n, :], xn)
        if has_ones:
            row = lax.broadcasted_iota(jnp.int32, val.shape, 0)
            is_ones = row == ONES_ROW
            for g in range(1, NSA_KV_HEADS):
                is_ones = is_ones | (row == g * V_ROWS + ONES_ROW)
            val = jnp.where(is_ones, 1.0, val)
        ref[...] = val.astype(BF16)
        off += n


def _inproj(h, gains, w_rows, w_cols, feats, layer):
    t = h.shape[0]
    tiles_per_seq = feats.shape[0] // PROJ_TM
    return pl.pallas_call(
        _inproj_body,
        grid=(t // PROJ_TM,),
        in_specs=[
            pl.BlockSpec((PROJ_TM, D_MODEL), lambda r: (r, 0)),
            pl.BlockSpec((None, 4, D_MODEL), lambda r: (layer, 0, 0)),
            _resident((D_MODEL, ROW_WIDTH), lambda r: (0, 0)),
            _resident((COL_WIDTH, D_MODEL), lambda r: (0, 0)),
            pl.BlockSpec((PROJ_TM, FEAT_WIDTH), lambda r: (lax.rem(r, tiles_per_seq), 0)),
        ],
        out_specs=([pl.BlockSpec((PROJ_TM, seg[0]), lambda r: (r, 0)) for seg in ROW_SEGS]
                   + [pl.BlockSpec((seg[0], PROJ_TM), lambda r: (0, r)) for seg in COL_SEGS]),
        out_shape=([jax.ShapeDtypeStruct((t, seg[0]), seg[1]) for seg in ROW_SEGS]
                   + [jax.ShapeDtypeStruct((seg[0], t), BF16) for seg in COL_SEGS]),
        compiler_params=_params("parallel"),
        name="inproj",
    )(h, gains, w_rows, w_cols, feats)


N_CMP_PAD = 128
CMP_HALF = CMP_BLOCK // CMP_STRIDE


def _prep_cmp(pe, w1, w2):
    pe2 = jnp.concatenate([pe, pe], axis=-1)
    w1r = w1.reshape(2, CMP_HALF, CMP_STRIDE, HEAD_DIM, CMP_HIDDEN)
    w1d = jnp.concatenate([w1r, w1r], axis=3).reshape(2, CMP_HALF, CMP_STRIDE * LANES, CMP_HIDDEN)
    w2k = jnp.stack([jnp.pad(w2[0], ((0, 0), (g * LANES, KV_PAD - g * LANES - HEAD_DIM)))
                     for g in range(NSA_KV_HEADS)])
    return pe2, w1d.astype(BF16), w2k.astype(BF16), w2[1].T.astype(BF16)


def _compress_body(kc_ref, vc_ref, pe_ref, w1_ref, w2k_ref, w2v_ref, ok_ref, ov_ref):
    lane = lax.broadcasted_iota(jnp.int32, (N_CMP_PAD, LANES), 1)
    out_k = jnp.zeros((N_CMP_PAD, KV_PAD), F32)
    for kv in range(2):
        src = (kc_ref, vc_ref)[kv]
        pieces = [src[pl.ds(l, N_CMP_PAD, stride=CMP_STRIDE), :] for l in range(CMP_STRIDE)]
        for g in range(NSA_KV_HEADS):
            in_group = (lane >= g * HEAD_DIM) & (lane < (g + 1) * HEAD_DIM)
            pre = None
            for half in range(CMP_HALF):
                xs = [jnp.where(in_group, pieces[l] + pe_ref[kv, half * CMP_STRIDE + l:half * CMP_STRIDE + l + 1, :], 0.0)
                      for l in range(CMP_STRIDE)]
                xcat = jnp.concatenate(xs, axis=-1).astype(BF16)
                part = _dot(xcat, w1_ref[kv, half])
                if half == 1:
                    part = pltpu.roll(part, N_CMP_PAD - 1, 0)
                pre = part if pre is None else pre + part
            hid = _silu(pre).astype(BF16)
            if kv == 0:
                out_k = out_k + _dot(hid, w2k_ref[g])
            else:
                ov_ref[g * HEAD_DIM:(g + 1) * HEAD_DIM, :] = _dot_nt(w2v_ref[...], hid).astype(BF16)
    ok_ref[...] = out_k.astype(BF16)


def _compress(kvc, pe2, w1d, w2k, w2v, b, s):
    return pl.pallas_call(
        _compress_body,
        grid=(b,),
        in_specs=[
            pl.BlockSpec((s, LANES), lambda r: (r, 0)),
            pl.BlockSpec((s, LANES), lambda r: (r, 1)),
            pl.BlockSpec(pe2.shape, lambda r: (0, 0, 0)),
            pl.BlockSpec(w1d.shape, lambda r: (0, 0, 0, 0)),
            pl.BlockSpec(w2k.shape, lambda r: (0, 0, 0)),
            pl.BlockSpec(w2v.shape, lambda r: (0, 0)),
        ],
        out_specs=[pl.BlockSpec((None, N_CMP_PAD, KV_PAD), lambda r: (r, 0, 0)),
                   pl.BlockSpec((None, NSA_KV_HEADS * HEAD_DIM, N_CMP_PAD), lambda r: (r, 0, 0))],
        out_shape=[jax.ShapeDtypeStruct((b, N_CMP_PAD, KV_PAD), BF16),
                   jax.ShapeDtypeStruct((b, NSA_KV_HEADS * HEAD_DIM, N_CMP_PAD), BF16)],
        compiler_params=_params("parallel"),
        name="compress",
    )(kvc, kvc, pe2, w1d, w2k, w2v)


ATT_T = 256
SEL_SHIFT = SEL_BLOCK.bit_length() - 1
MASK_DIST = 2.0 ** 100
MASK_BIAS = -(2.0 ** 100)
WIN_KEYS = WINDOW + ATT_T
DIAG_KEYS = 2 * ATT_T
HEAD_PACK = 4


def _overlap_t(s):
    n_cmp = (s - CMP_BLOCK) // CMP_STRIDE + 1
    n_sel = s // SEL_BLOCK
    assert n_cmp <= N_CMP_PAD
    cs = np.arange(n_cmp)[None, :] * CMP_STRIDE
    ss = np.arange(n_sel)[:, None] * SEL_BLOCK
    ov = np.clip(np.minimum(cs + CMP_BLOCK, ss + SEL_BLOCK) - np.maximum(cs, ss), 0, None) / CMP_BLOCK
    return jnp.asarray(np.pad(ov, ((0, 0), (0, N_CMP_PAD - n_cmp))), BF16)


def _softmax_pv_t(sc, v_t):
    p = jnp.exp(sc - jnp.max(sc, axis=0, keepdims=True))
    o = _dot(v_t, p.astype(BF16))
    return o[0:HEAD_DIM, :] * (1.0 / o[ONES_ROW:ONES_ROW + 1, :])


def _attn_body(q_ref, kc_ref, vc_ref, ks_ref, vs_ref, kw_ref, vw_ref, gl_ref, ovt_ref, o_ref, out_ref,
               *, n_sel, n_variants):
    i = pl.program_id(1)
    t0 = i * ATT_T
    gates = jax.nn.sigmoid(gl_ref[...].T)

    def gate(hh, branch):
        return gates[3 * hh + branch:3 * hh + branch + 1, :]

    def rel(n):
        return lax.broadcasted_iota(jnp.int32, (n, ATT_T), 1) - lax.broadcasted_iota(jnp.int32, (n, ATT_T), 0)

    w0 = pl.multiple_of(jnp.maximum(t0 - WINDOW, 0), ATT_T)
    dist_w = (t0 - w0) + rel(WIN_KEYS)
    bias_w = jnp.where((dist_w >= 0) & (dist_w < WINDOW), 0.0, MASK_BIAS)

    blk_c = lax.broadcasted_iota(jnp.int32, (N_CMP_PAD, ATT_T), 0)
    d_cmp = t0 + lax.broadcasted_iota(jnp.int32, (N_CMP_PAD, ATT_T), 1) - (blk_c * CMP_STRIDE + (CMP_BLOCK - 1))
    valid_c = d_cmp >= 0
    dm_c = jnp.where(valid_c, d_cmp.astype(F32), MASK_DIST)

    blk = lax.broadcasted_iota(jnp.int32, (n_sel, ATT_T), 0)
    tq = t0 + lax.broadcasted_iota(jnp.int32, (n_sel, ATT_T), 1)
    cur = tq >> SEL_SHIFT
    forced = (blk == 0) | (blk == cur) | (blk == cur - 1)
    eligible = blk * SEL_BLOCK <= tq
    feat_row = lax.broadcasted_iota(jnp.int32, (LANES - BLOCK_LANE - n_sel, ATT_T), 0)

    def lanes(parts):
        return jnp.concatenate(parts, axis=1)

    def tiled(a):
        return lanes([a] * HEAD_PACK)

    bias_w = tiled(bias_w)
    dm_c = tiled(dm_c)
    valid_c = tiled(valid_c)
    pad = jnp.zeros((LANES - HEAD_DIM, ATT_T), BF16)

    for g in range(NSA_KV_HEADS):
        group = slice(g * LANES, (g + 1) * LANES)
        v_rows = slice(g * V_ROWS, (g + 1) * V_ROWS)
        packs = [[g * NSA_GROUP + k + h for h in range(HEAD_PACK)] for k in range(0, NSA_GROUP, HEAD_PACK)]
        pack_ids = [heads[0] // HEAD_PACK for heads in packs]
        slopes = [[2.0 ** -(hh + 1) for hh in heads] for heads in packs]
        qs = [[q_ref[hh * HEAD_DIM:(hh + 1) * HEAD_DIM, :] * (HEAD_DIM ** -0.5) for hh in heads] for heads in packs]

        def gate_row(heads, branch):
            return lanes([gate(hh, branch) for hh in heads])

        kc = kc_ref[:, group]
        vc_t = vc_ref[g * HEAD_DIM:(g + 1) * HEAD_DIM, :]
        psum = jnp.zeros((N_CMP_PAD, ATT_T), F32)
        for pk, heads in enumerate(packs):
            slope_row = lanes([jnp.full((1, ATT_T), sl, F32) for sl in slopes[pk]])
            sc = _dot(kc, lanes([jnp.concatenate([q, pad], axis=0) for q in qs[pk]])) - slope_row * dm_c
            e = jnp.exp(sc - jnp.max(sc, axis=0, keepdims=True))
            p = e * (1.0 / jnp.sum(e, axis=0, keepdims=True))
            p = jnp.where(valid_c, p, 0.0)
            for h in range(HEAD_PACK):
                psum = psum + p[:, h * ATT_T:(h + 1) * ATT_T]
            out_ref[pack_ids[pk]] = gate_row(heads, 0) * _dot(vc_t, p.astype(BF16))

        p_hi = psum.astype(BF16)
        p_lo = (psum - p_hi.astype(F32)).astype(BF16)
        imp = _dot(ovt_ref[...], p_hi) + _dot(ovt_ref[...], p_lo)
        imp = jnp.where(eligible, imp + jnp.where(forced, FORCE_BONUS, 0.0), -1.0)
        rank = jnp.zeros((n_sel, ATT_T), F32)
        for j in range(n_sel):
            cj = imp[j:j + 1, :]
            tie = jnp.where(blk > j, 1.0, 0.0)
            rank = rank + jnp.where(cj > imp, 1.0, jnp.where(cj == imp, tie, 0.0))
        drop = jnp.where(rank < min(SEL_TOPK, n_sel), 0.0, MASK_BIAS).astype(BF16)

        q_aug = []
        for pk in range(len(packs)):
            cols = []
            for q, sl in zip(qs[pk], slopes[pk]):
                alibi = jnp.where(feat_row == ALIBI_LANE - BLOCK_LANE - n_sel, sl * SEL_BLOCK,
                                  jnp.where(feat_row == ALIBI_LANE + 1 - BLOCK_LANE - n_sel, sl, 0.0))
                cols.append(jnp.concatenate([q, drop, alibi.astype(BF16)], axis=0))
            q_aug.append(lanes(cols))

        def selected(n_keys):
            k_sel = ks_ref[0:n_keys, group]
            v_sel = vs_ref[v_rows, 0:n_keys]
            causal = tiled(jnp.where(rel(DIAG_KEYS) + (t0 - (n_keys - DIAG_KEYS)) >= 0, 0.0, MASK_BIAS))
            for pk, heads in enumerate(packs):
                sc = _dot(k_sel, q_aug[pk])
                tail = sc[n_keys - DIAG_KEYS:, :] + causal
                sc = tail if n_keys == DIAG_KEYS else jnp.concatenate([sc[:n_keys - DIAG_KEYS, :], tail], axis=0)
                out_ref[pack_ids[pk]] += gate_row(heads, 1) * _softmax_pv_t(sc, v_sel)

        for v in range(n_variants):
            pl.when(lax.shift_right_logical(i, 1) == v)(functools.partial(selected, (v + 1) * DIAG_KEYS))

        k_win = kw_ref[pl.ds(w0, WIN_KEYS), group]
        v_win = vw_ref[v_rows, pl.ds(w0, WIN_KEYS)]
        for pk, heads in enumerate(packs):
            out_ref[pack_ids[pk]] += gate_row(heads, 2) * _softmax_pv_t(_dot(k_win, q_aug[pk]) + bias_w, v_win)

    o_t = [out_ref[hh // HEAD_PACK][:, (hh % HEAD_PACK) * ATT_T:(hh % HEAD_PACK + 1) * ATT_T]
           for hh in range(NSA_HEADS)]
    o_ref[...] = jnp.concatenate(o_t, axis=0).T


def _attention(q_t, kcb, vcb_t, ks, vs_t, kw, vw_t, gldt, ovt, b, s):
    nq = s // ATT_T
    n_sel = s // SEL_BLOCK
    assert nq % 2 == 0
    keys = pl.BlockSpec((s, KV_PAD), lambda r, i: (r, 0))
    vals_t = pl.BlockSpec((NSA_KV_HEADS * V_ROWS, s), lambda r, i: (0, r))
    return pl.pallas_call(
        functools.partial(_attn_body, n_sel=n_sel, n_variants=nq // 2),
        grid=(b, nq),
        in_specs=[
            pl.BlockSpec((NSA_HEADS * HEAD_DIM, ATT_T), lambda r, i: (0, r * nq + i)),
            pl.BlockSpec((None, N_CMP_PAD, KV_PAD), lambda r, i: (r, 0, 0)),
            pl.BlockSpec((None, NSA_KV_HEADS * HEAD_DIM, N_CMP_PAD), lambda r, i: (r, 0, 0)),
            keys, vals_t, keys, vals_t,
            pl.BlockSpec((ATT_T, LANES), lambda r, i: (r * nq + i, 0)),
            pl.BlockSpec(ovt.shape, lambda r, i: (0, 0)),
        ],
        out_specs=pl.BlockSpec((ATT_T, NSA_HEADS * HEAD_DIM), lambda r, i: (r * nq + i, 0)),
        out_shape=jax.ShapeDtypeStruct((b * s, NSA_HEADS * HEAD_DIM), F32),
        scratch_shapes=[pltpu.VMEM((NSA_HEADS // HEAD_PACK, HEAD_DIM, HEAD_PACK * ATT_T), F32)],
        compiler_params=_params("parallel", "parallel"),
        name="nsa_attention",
    )(q_t, kcb, vcb_t, ks, vs_t, kw, vw_t, gldt, ovt)


CONV_PAD = 8
DT_LANE = 3 * NSA_HEADS


def _split_dot(a, b):
    b0 = b.astype(BF16)
    r1 = b - b0.astype(F32)
    b1 = r1.astype(BF16)
    b2 = (r1 - b1.astype(F32)).astype(BF16)
    return _dot(a, b0) + _dot(a, b1) + _dot(a, b2)


def _split_dot_rhs(a, b):
    a0 = a.astype(BF16)
    r1 = a - a0.astype(F32)
    a1 = r1.astype(BF16)
    a2 = (r1 - a1.astype(F32)).astype(BF16)
    return _dot(a0, b) + _dot(a1, b) + _dot(a2, b)


def _softplus(x):
    return jnp.maximum(x, 0.0) + jnp.log1p(jnp.exp(-jnp.abs(x)))


def _ssd_body(xbc_ref, z_ref, gldt_ref, dte_ref, cw_ref, cb_ref, dtb_e_ref, alog_e_ref, dtb_c_ref, alog_c_ref,
              d_ref, nw_ref, o_ref, state_ref, xe_ref):
    c = pl.program_id(1)
    L = SSM_CHUNK

    @pl.when(c == 0)
    def _():
        state_ref[...] = jnp.zeros(state_ref.shape, F32)
        xe_ref[0:CONV_PAD, :] = jnp.zeros((CONV_PAD, SSM_CONV_DIM), F32)

    x_raw = xbc_ref[...]
    xe_ref[CONV_PAD:CONV_PAD + L, :] = x_raw
    acc = jnp.zeros((L, SSM_CONV_DIM), F32)
    for k in range(SSM_CONV):
        lo = CONV_PAD - (SSM_CONV - 1) + k
        acc = acc + xe_ref[lo:lo + L, :] * cw_ref[k:k + 1, :]
    xe_ref[0:CONV_PAD, :] = x_raw[L - CONV_PAD:, :]
    xa = _silu(acc + cb_ref[...])
    xs = xa[:, :SSM_D_INNER]
    bm = xa[:, SSM_D_INNER:SSM_D_INNER + SSM_BC_DIM]
    cm = xa[:, SSM_D_INNER + SSM_BC_DIM:]

    row = lax.broadcasted_iota(jnp.int32, (L, L), 0)
    col = lax.broadcasted_iota(jnp.int32, (L, L), 1)
    causal = col <= row
    tri = jnp.where(causal, 1.0, 0.0).astype(BF16)
    tri_t = jnp.where(row <= col, 1.0, 0.0).astype(BF16)

    dt = _softplus(dte_ref[...] + dtb_e_ref[...])
    a = dt * (-jnp.exp(alog_e_ref[...]))
    a_cs = _split_dot(tri, a)
    a_last = a_cs[L - 1:L, :]
    dt_t = _softplus(gldt_ref[...].T[DT_LANE:DT_LANE + SSM_HEADS, :] + dtb_c_ref[...])
    a_cs_t = _split_dot_rhs(dt_t * (-jnp.exp(alog_c_ref[...])), tri_t)

    xdt = xs * dt
    xw = (xdt * jnp.exp(a_last - a_cs)).astype(BF16)
    exp_cs = jnp.exp(a_cs)
    lane = lax.broadcasted_iota(jnp.int32, (L, LANES), 1)
    rpg = SSM_HEADS // SSM_GROUPS
    gw = rpg * SSM_HEAD_DIM
    ys = []
    for g in range(SSM_GROUPS):
        bm_g = bm[:, g * SSM_STATE:(g + 1) * SSM_STATE]
        cm_g = cm[:, g * SSM_STATE:(g + 1) * SSM_STATE].astype(BF16)
        cb = _dot_nt(cm_g, bm_g.astype(BF16))
        prev = state_ref[:, g * gw:(g + 1) * gw]
        y_off = _dot(cm_g, prev.astype(BF16)) * exp_cs[:, g * gw:(g + 1) * gw]
        new_state = _dot(bm_g.T.astype(BF16), xw[:, g * gw:(g + 1) * gw])
        state_ref[:, g * gw:(g + 1) * gw] = prev * jnp.exp(a_last[:, g * gw:(g + 1) * gw]) + new_state
        for pair in range(rpg // 2):
            h0 = g * rpg + 2 * pair
            x_pair = xdt[:, h0 * SSM_HEAD_DIM:(h0 + 2) * SSM_HEAD_DIM].astype(BF16)
            halves = []
            for h in (h0, h0 + 1):
                diff = a_cs[:, h * SSM_HEAD_DIM:h * SSM_HEAD_DIM + 1] - a_cs_t[h:h + 1, :]
                decay = jnp.exp(jnp.where(causal, diff, NEG_INF))
                halves.append(_dot((cb * decay).astype(BF16), x_pair))
            ys.append(jnp.where(lane < SSM_HEAD_DIM, halves[0], halves[1])
                      + y_off[:, 2 * pair * SSM_HEAD_DIM:(2 * pair + 2) * SSM_HEAD_DIM])
    y = jnp.concatenate(ys, axis=-1) + xs * d_ref[...]
    y = y * _silu(z_ref[...])
    outs = []
    for g in range(SSM_GROUPS):
        yg = y[:, g * gw:(g + 1) * gw]
        outs.append(_rms(yg, nw_ref[:, g * gw:(g + 1) * gw]))
    o_ref[...] = jnp.concatenate(outs, axis=-1)


def _ssd(xbc, z, gldt, dte, conv_w, conv_b, dtb_e, alog_e, dtb_c, alog_c, d_e, norm_w, b, s):
    nc = s // SSM_CHUNK
    rows = lambda n: pl.BlockSpec((SSM_CHUNK, n), lambda r, c: (r * nc + c, 0))
    full = lambda arr: pl.BlockSpec(arr.shape, lambda r, c: (0,) * arr.ndim)
    consts = (conv_w, conv_b, dtb_e, alog_e, dtb_c, alog_c, d_e, norm_w)
    return pl.pallas_call(
        _ssd_body,
        grid=(b, nc),
        in_specs=[rows(SSM_CONV_DIM), rows(SSM_D_INNER), rows(LANES), rows(SSM_D_INNER)] + [full(a) for a in consts],
        out_specs=rows(SSM_D_INNER),
        out_shape=jax.ShapeDtypeStruct((b * s, SSM_D_INNER), F32),
        scratch_shapes=[
            pltpu.VMEM((SSM_STATE, SSM_D_INNER), F32),
            pltpu.VMEM((CONV_PAD + SSM_CHUNK, SSM_CONV_DIM), F32),
        ],
        compiler_params=_params("parallel", "arbitrary"),
        name="ssd",
    )(xbc, z, gldt, dte, *consts)


def _outproj_body(h_ref, oa_ref, os_ref, g_ref, w_ref, o_ref):
    na = oa_ref.shape[-1]
    y = _dot(oa_ref[...].astype(BF16), w_ref[:na, :]) + _dot(os_ref[...].astype(BF16), w_ref[na:, :])
    o_ref[...] = h_ref[...] + _rms(y, g_ref[1:2, :])


def _outproj(h, oa, os_, gains, w, layer):
    t = h.shape[0]
    rows = lambda n: pl.BlockSpec((PROJ_TM, n), lambda r: (r, 0))
    return pl.pallas_call(
        _outproj_body,
        grid=(t // PROJ_TM,),
        in_specs=[
            rows(D_MODEL), rows(oa.shape[1]), rows(os_.shape[1]),
            pl.BlockSpec((None, 4, D_MODEL), lambda r: (layer, 0, 0)),
            pl.BlockSpec(w.shape, lambda r: (0, 0)),
        ],
        out_specs=rows(D_MODEL),
        out_shape=jax.ShapeDtypeStruct((t, D_MODEL), F32),
        compiler_params=_params("parallel"),
        name="outproj",
    )(h, oa, os_, gains, w)


def _even_mixer(h, gains, layer, w_in, w_out, pe, w1, w2, conv_w, conv_b, dt_bias, a_log, d_skip, norm_w, b, s):
    w_rows, w_cols = _prep_w_in(w_in)
    kvc, ks, kw, z, xbc, gldt, dte, q_t, vs_t, vw_t = _inproj(h, gains, w_rows, w_cols, _position_features(s), layer)
    kcb, vcb_t = _compress(kvc, *_prep_cmp(pe, w1, w2), b, s)
    o_attn = _attention(q_t, kcb, vcb_t, ks, vs_t, kw, vw_t, gldt, _overlap_t(s), b, s)
    rep = lambda v: jnp.repeat(v, SSM_HEAD_DIM)[None, :]
    o_ssm = _ssd(xbc, z, gldt, dte, conv_w, conv_b[None, :], rep(dt_bias), rep(a_log), dt_bias[:, None],
                 a_log[:, None], rep(d_skip), norm_w[None, :], b, s)
    return _outproj(h, o_attn, o_ssm, gains, w_out.astype(BF16), layer)


def kernel(x, norm_gains, ffn_w_gate, ffn_w_up, ffn_w_down, ev_w_in, ev_w_out, nsa_cmp_pe, nsa_cmp_w1, nsa_cmp_w2,
           ssm_conv_w, ssm_conv_b, ssm_dt_bias, ssm_a_log, ssm_d, ssm_norm_w, od_w_in, od_ln_w, od_ln_b, od_w_s,
           od_b_s, od_w_out):
    b, s, d = x.shape
    depth = norm_gains.shape[0]
    h = x.reshape(b * s, d)
    wg = ffn_w_gate.astype(BF16)
    wu = ffn_w_up.astype(BF16)
    wd = ffn_w_down.astype(BF16)
    od_in = od_w_in.astype(BF16)
    od_out = od_w_out.astype(BF16)
    od_lnw = od_ln_w[:, None, :]
    od_lnb = od_ln_b[:, None, :]
    od_bst = jnp.swapaxes(od_b_s, 1, 2)
    for layer in range(depth):
        i = layer // 2
        if layer % 2 == 0:
            h = _even_mixer(h, norm_gains, layer, ev_w_in[i], ev_w_out[i], nsa_cmp_pe[i], nsa_cmp_w1[i],
                            nsa_cmp_w2[i], ssm_conv_w[i], ssm_conv_b[i], ssm_dt_bias[i], ssm_a_log[i], ssm_d[i],
                            ssm_norm_w[i], b, s)
        else:
            h = _gmlp(h, norm_gains, od_in, od_lnw, od_lnb, od_w_s, od_bst, od_out, layer, i)
        h = _ffn(h, norm_gains, wg, wu, wd, layer)
    return h.reshape(b, s, d)
```

```python
import functools

import numpy as np
import jax
import jax.numpy as jnp
from jax import lax
from jax.experimental import pallas as pl
from jax.experimental.pallas import tpu as pltpu

F32 = jnp.float32
BF16 = jnp.bfloat16

D_MODEL = 1024
RMS_EPS = 1e-6
LN_EPS = 1e-5
NEG_INF = -1e30

NSA_HEADS = 8
NSA_KV_HEADS = 2
NSA_GROUP = NSA_HEADS // NSA_KV_HEADS
HEAD_DIM = 64
CMP_BLOCK = 32
CMP_STRIDE = 16
CMP_HIDDEN = 256
SEL_BLOCK = 64
SEL_TOPK = 16
WINDOW = 512
FORCE_BONUS = 1e4

SSM_HEADS = 8
SSM_HEAD_DIM = 64
SSM_D_INNER = SSM_HEADS * SSM_HEAD_DIM
SSM_GROUPS = 2
SSM_STATE = 128
SSM_CONV = 4
SSM_CHUNK = 128
SSM_BC_DIM = SSM_GROUPS * SSM_STATE
SSM_CONV_DIM = SSM_D_INNER + 2 * SSM_BC_DIM

GMLP_WIDTH = 2 * D_MODEL
GMLP_GROUPS = 8
GMLP_GROUP_DIM = GMLP_WIDTH // GMLP_GROUPS
GMLP_CHUNK = 128

LANES = 128
VMEM_LIMIT = 56 * 1024 * 1024


def _params(*sem):
    return pltpu.CompilerParams(dimension_semantics=sem, vmem_limit_bytes=VMEM_LIMIT)


def _rms(x, g):
    return x * lax.rsqrt(jnp.mean(x * x, axis=-1, keepdims=True) + RMS_EPS) * g


def _silu(x):
    return x * jax.nn.sigmoid(x)


def _dot(a, b):
    return jnp.dot(a, b, preferred_element_type=F32)


def _dot_nt(a, b):
    return lax.dot_general(a, b, (((1,), (1,)), ((), ())), preferred_element_type=F32)


def _resident(block_shape, index_map):
    return pl.BlockSpec(block_shape, index_map, pipeline_mode=pl.Buffered(1))


FFN_TM = 512
FFN_CHUNK = 256


def _ffn_body(h_ref, g_ref, wg_ref, wu_ref, wd_ref, o_ref):
    x = h_ref[...]
    xn = _rms(x, g_ref[2:3, :]).astype(BF16)
    acc = jnp.zeros(x.shape, F32)
    for c in range(wg_ref.shape[1] // FFN_CHUNK):
        cols = slice(c * FFN_CHUNK, (c + 1) * FFN_CHUNK)
        a = _dot(xn, wg_ref[:, cols])
        b = _dot(xn, wu_ref[:, cols])
        acc = acc + _dot((_silu(a) * b).astype(BF16), wd_ref[cols, :])
    o_ref[...] = x + _rms(acc, g_ref[3:4, :])


def _ffn(h, gains, wg, wu, wd, layer):
    t = h.shape[0]
    hidden = wg.shape[2]
    return pl.pallas_call(
        _ffn_body,
        grid=(t // FFN_TM,),
        in_specs=[
            pl.BlockSpec((FFN_TM, D_MODEL), lambda i: (i, 0)),
            pl.BlockSpec((None, 4, D_MODEL), lambda i: (layer, 0, 0)),
            _resident((None, D_MODEL, hidden), lambda i: (layer, 0, 0)),
            _resident((None, D_MODEL, hidden), lambda i: (layer, 0, 0)),
            _resident((None, hidden, D_MODEL), lambda i: (layer, 0, 0)),
        ],
        out_specs=pl.BlockSpec((FFN_TM, D_MODEL), lambda i: (i, 0)),
        out_shape=jax.ShapeDtypeStruct((t, D_MODEL), F32),
        compiler_params=_params("parallel"),
        name="ffn",
    )(h, gains, wg, wu, wd)


GMLP_TM = 512


def _gelu(x):
    return 0.5 * x * (1.0 + lax.erf(x * np.float32(1.0 / np.sqrt(2.0))))


def _gmlp_body(h_ref, g_ref, win_ref, lnw_ref, lnb_ref, ws_ref, bs_ref, wout_ref, o_ref, gate_ref):
    xn = _rms(h_ref[...], g_ref[0:1, :]).astype(BF16)
    uv = _gelu(_dot(xn, win_ref[...]))
    u = uv[:, :GMLP_WIDTH]
    v = uv[:, GMLP_WIDTH:]
    mu = jnp.mean(v, axis=-1, keepdims=True)
    vc = v - mu
    var = jnp.mean(vc * vc, axis=-1, keepdims=True)
    v = (vc * lax.rsqrt(var + LN_EPS) * lnw_ref[...] + lnb_ref[...]).astype(BF16)
    row = lax.broadcasted_iota(jnp.int32, (GMLP_CHUNK, GMLP_CHUNK), 0)
    col = lax.broadcasted_iota(jnp.int32, (GMLP_CHUNK, GMLP_CHUNK), 1)
    causal = col <= row
    for g in range(GMLP_GROUPS):
        ws = jnp.where(causal, ws_ref[g], 0.0).astype(BF16)
        bias = bs_ref[:, g:g + 1]
        cols = slice(g * GMLP_GROUP_DIM, (g + 1) * GMLP_GROUP_DIM)
        for c in range(GMLP_TM // GMLP_CHUNK):
            rows = slice(c * GMLP_CHUNK, (c + 1) * GMLP_CHUNK)
            mixed = _dot(ws, v[rows, cols]) + bias
            gate_ref[rows, cols] = (u[rows, cols] * mixed).astype(BF16)
    y = _dot(gate_ref[...], wout_ref[...])
    o_ref[...] = h_ref[...] + _rms(y, g_ref[1:2, :])


def _gmlp(h, gains, w_in, ln_w, ln_b, w_s, b_s_t, w_out, layer, i):
    t = h.shape[0]
    const2 = lambda r: (i, 0, 0)
    return pl.pallas_call(
        _gmlp_body,
        grid=(t // GMLP_TM,),
        in_specs=[
            pl.BlockSpec((GMLP_TM, D_MODEL), lambda r: (r, 0)),
            pl.BlockSpec((None, 4, D_MODEL), lambda r: (layer, 0, 0)),
            _resident((None, D_MODEL, 2 * GMLP_WIDTH), const2),
            pl.BlockSpec((None, 1, GMLP_WIDTH), const2),
            pl.BlockSpec((None, 1, GMLP_WIDTH), const2),
            pl.BlockSpec((None, GMLP_GROUPS, GMLP_CHUNK, GMLP_CHUNK), lambda r: (i, 0, 0, 0)),
            pl.BlockSpec((None, GMLP_CHUNK, GMLP_GROUPS), const2),
            _resident((None, GMLP_WIDTH, D_MODEL), const2),
        ],
        out_specs=pl.BlockSpec((GMLP_TM, D_MODEL), lambda r: (r, 0)),
        out_shape=jax.ShapeDtypeStruct((t, D_MODEL), F32),
        scratch_shapes=[pltpu.VMEM((GMLP_TM, GMLP_WIDTH), BF16)],
        compiler_params=_params("parallel"),
        name="gmlp",
    )(h, gains, w_in, ln_w, ln_b, w_s, b_s_t, w_out)


PROJ_TM = 512
KV_PAD = NSA_KV_HEADS * LANES
BLOCK_LANE = HEAD_DIM
ALIBI_LANE = 96
V_ROWS = HEAD_DIM + 16
ONES_ROW = HEAD_DIM
ROW_SEGS = ((2 * LANES, F32, False),
            (KV_PAD, BF16, True),
            (KV_PAD, BF16, True),
            (SSM_D_INNER, F32, False),
            (SSM_CONV_DIM, F32, False),
            (LANES, F32, False),
            (SSM_D_INNER, F32, False))
COL_SEGS = ((NSA_HEADS * HEAD_DIM, False),
            (NSA_KV_HEADS * V_ROWS, True),
            (NSA_KV_HEADS * V_ROWS, True))
ROW_WIDTH = sum(seg[0] for seg in ROW_SEGS)
COL_WIDTH = sum(seg[0] for seg in COL_SEGS)
FEAT_WIDTH = sum(seg[0] for seg in ROW_SEGS if seg[2])


def _prep_w_in(w):
    nq = NSA_HEADS * HEAD_DIM
    nkv = NSA_KV_HEADS * HEAD_DIM

    def pad_groups(a, width):
        a = a.reshape(D_MODEL, NSA_KV_HEADS, HEAD_DIM)
        return jnp.pad(a, ((0, 0), (0, 0), (0, width - HEAD_DIM))).reshape(D_MODEL, NSA_KV_HEADS * width)

    o = nq
    kc, vc, ks, vs, kw, vw = [w[:, o + j * nkv:o + (j + 1) * nkv] for j in range(6)]
    o += 6 * nkv
    gl = w[:, o:o + 3 * NSA_HEADS]
    o += 3 * NSA_HEADS
    z = w[:, o:o + SSM_D_INNER]
    o += SSM_D_INNER
    xbc = w[:, o:o + SSM_CONV_DIM]
    o += SSM_CONV_DIM
    dt = w[:, o:o + SSM_HEADS]
    gldt = jnp.concatenate([gl, dt, jnp.zeros((D_MODEL, LANES - 3 * NSA_HEADS - SSM_HEADS), w.dtype)], axis=1)
    dte = jnp.repeat(dt, SSM_HEAD_DIM, axis=1)
    rows = jnp.concatenate([kc, vc, pad_groups(ks, LANES), pad_groups(kw, LANES), z, xbc, gldt, dte], axis=1)
    cols = jnp.concatenate([w[:, :nq], pad_groups(vs, V_ROWS), pad_groups(vw, V_ROWS)], axis=1).T
    assert rows.shape[1] == ROW_WIDTH and cols.shape[0] == COL_WIDTH
    return rows.astype(BF16), cols.astype(BF16)


def _position_features(s):
    n_sel = s // SEL_BLOCK
    assert BLOCK_LANE + n_sel <= ALIBI_LANE and ALIBI_LANE + 2 <= LANES
    pos = np.arange(s)
    alibi = np.zeros((s, LANES), np.float32)
    alibi[:, ALIBI_LANE] = pos // SEL_BLOCK
    alibi[:, ALIBI_LANE + 1] = pos % SEL_BLOCK
    k_sel = alibi.copy()
    k_sel[pos, BLOCK_LANE + pos // SEL_BLOCK] = 1.0
    feats = [np.tile(a, (1, NSA_KV_HEADS)) for a in (k_sel, alibi)]
    return jnp.asarray(np.concatenate(feats, axis=1), BF16)


def _inproj_body(h_ref, g_ref, wr_ref, wc_ref, feat_ref, *out_refs):
    xn = _rms(h_ref[...], g_ref[0:1, :]).astype(BF16)
    row_refs = out_refs[:len(ROW_SEGS)]
    col_refs = out_refs[len(ROW_SEGS):]
    off = 0
    feat_off = 0
    for ref, (n, dtype, has_feat) in zip(row_refs, ROW_SEGS):
        val = _dot(xn, wr_ref[:, off:off + n]).astype(dtype)
        if has_feat:
            val = val + feat_ref[:, feat_off:feat_off + n]
            feat_off += n
        ref[...] = val
        off += n
    off = 0
    for ref, (n, has_ones) in zip(col_refs, COL_SEGS):
        val = _dot_nt(wc_ref[off:off + n, :], xn)
        if has_ones:
            row = lax.broadcasted_iota(jnp.int32, val.shape, 0)
            is_ones = row == ONES_ROW
            for g in range(1, NSA_KV_HEADS):
                is_ones = is_ones | (row == g * V_ROWS + ONES_ROW)
            val = jnp.where(is_ones, 1.0, val)
        ref[...] = val.astype(BF16)
        off += n


def _inproj(h, gains, w_rows, w_cols, feats, layer):
    t = h.shape[0]
    tiles_per_seq = feats.shape[0] // PROJ_TM
    return pl.pallas_call(
        _inproj_body,
        grid=(t // PROJ_TM,),
        in_specs=[
            pl.BlockSpec((PROJ_TM, D_MODEL), lambda r: (r, 0)),
            pl.BlockSpec((None, 4, D_MODEL), lambda r: (layer, 0, 0)),
            _resident((D_MODEL, ROW_WIDTH), lambda r: (0, 0)),
            _resident((COL_WIDTH, D_MODEL), lambda r: (0, 0)),
            pl.BlockSpec((PROJ_TM, FEAT_WIDTH), lambda r: (lax.rem(r, tiles_per_seq), 0)),
        ],
        out_specs=([pl.BlockSpec((PROJ_TM, seg[0]), lambda r: (r, 0)) for seg in ROW_SEGS]
                   + [pl.BlockSpec((seg[0], PROJ_TM), lambda r: (0, r)) for seg in COL_SEGS]),
        out_shape=([jax.ShapeDtypeStruct((t, seg[0]), seg[1]) for seg in ROW_SEGS]
                   + [jax.ShapeDtypeStruct((seg[0], t), BF16) for seg in COL_SEGS]),
        compiler_params=_params("parallel"),
        name="inproj",
    )(h, gains, w_rows, w_cols, feats)


N_CMP_PAD = 128
CMP_HALF = CMP_BLOCK // CMP_STRIDE


def _prep_cmp(pe, w1, w2):
    pe2 = jnp.concatenate([pe, pe], axis=-1)
    w1r = w1.reshape(2, CMP_HALF, CMP_STRIDE, HEAD_DIM, CMP_HIDDEN)
    w1d = jnp.concatenate([w1r, w1r], axis=3).reshape(2, CMP_HALF, CMP_STRIDE * LANES, CMP_HIDDEN)
    w2k = jnp.stack([jnp.pad(w2[0], ((0, 0), (g * LANES, KV_PAD - g * LANES - HEAD_DIM)))
                     for g in range(NSA_KV_HEADS)])
    return pe2, w1d.astype(BF16), w2k.astype(BF16), w2[1].T.astype(BF16)


def _compress_body(kc_ref, vc_ref, pe_ref, w1_ref, w2k_ref, w2v_ref, ok_ref, ov_ref):
    lane = lax.broadcasted_iota(jnp.int32, (N_CMP_PAD, LANES), 1)
    out_k = jnp.zeros((N_CMP_PAD, KV_PAD), F32)
    for kv in range(2):
        src = (kc_ref, vc_ref)[kv]
        pieces = [src[pl.ds(l, N_CMP_PAD, stride=CMP_STRIDE), :] for l in range(CMP_STRIDE)]
        for g in range(NSA_KV_HEADS):
            in_group = (lane >= g * HEAD_DIM) & (lane < (g + 1) * HEAD_DIM)
            pre = None
            for half in range(CMP_HALF):
                xs = [jnp.where(in_group, pieces[l] + pe_ref[kv, half * CMP_STRIDE + l:half * CMP_STRIDE + l + 1, :], 0.0)
                      for l in range(CMP_STRIDE)]
                xcat = jnp.concatenate(xs, axis=-1).astype(BF16)
                part = _dot(xcat, w1_ref[kv, half])
                if half == 1:
                    part = pltpu.roll(part, N_CMP_PAD - 1, 0)
                pre = part if pre is None else pre + part
            hid = _silu(pre).astype(BF16)
            if kv == 0:
                out_k = out_k + _dot(hid, w2k_ref[g])
            else:
                ov_ref[g * HEAD_DIM:(g + 1) * HEAD_DIM, :] = _dot_nt(w2v_ref[...], hid).astype(BF16)
    ok_ref[...] = out_k.astype(BF16)


def _compress(kvc, pe2, w1d, w2k, w2v, b, s):
    return pl.pallas_call(
        _compress_body,
        grid=(b,),
        in_specs=[
            pl.BlockSpec((s, LANES), lambda r: (r, 0)),
            pl.BlockSpec((s, LANES), lambda r: (r, 1)),
            pl.BlockSpec(pe2.shape, lambda r: (0, 0, 0)),
            pl.BlockSpec(w1d.shape, lambda r: (0, 0, 0, 0)),
            pl.BlockSpec(w2k.shape, lambda r: (0, 0, 0)),
            pl.BlockSpec(w2v.shape, lambda r: (0, 0)),
        ],
        out_specs=[pl.BlockSpec((None, N_CMP_PAD, KV_PAD), lambda r: (r, 0, 0)),
                   pl.BlockSpec((None, NSA_KV_HEADS * HEAD_DIM, N_CMP_PAD), lambda r: (r, 0, 0))],
        out_shape=[jax.ShapeDtypeStruct((b, N_CMP_PAD, KV_PAD), BF16),
                   jax.ShapeDtypeStruct((b, NSA_KV_HEADS * HEAD_DIM, N_CMP_PAD), BF16)],
        compiler_params=_params("parallel"),
        name="compress",
    )(kvc, kvc, pe2, w1d, w2k, w2v)


ATT_T = 256
SEL_SHIFT = SEL_BLOCK.bit_length() - 1
MASK_DIST = 2.0 ** 100
MASK_BIAS = -(2.0 ** 100)
WIN_KEYS = WINDOW + ATT_T
DIAG_KEYS = 512
TILES_PER_DIAG = DIAG_KEYS // ATT_T
HEAD_PACK = 4


def _overlap_t(s):
    n_cmp = (s - CMP_BLOCK) // CMP_STRIDE + 1
    n_sel = s // SEL_BLOCK
    assert n_cmp <= N_CMP_PAD
    cs = np.arange(n_cmp)[None, :] * CMP_STRIDE
    ss = np.arange(n_sel)[:, None] * SEL_BLOCK
    ov = np.clip(np.minimum(cs + CMP_BLOCK, ss + SEL_BLOCK) - np.maximum(cs, ss), 0, None) / CMP_BLOCK
    return jnp.asarray(np.pad(ov, ((0, 0), (0, N_CMP_PAD - n_cmp))), BF16)


def _softmax_pv_t(sc, v_t):
    p = jnp.exp(sc - jnp.max(sc, axis=0, keepdims=True))
    o = _dot(v_t, p.astype(BF16))
    return o[0:HEAD_DIM, :] * (1.0 / o[ONES_ROW:ONES_ROW + 1, :])


def _attn_body(q_ref, kc_ref, vc_ref, ks_ref, vs_ref, kw_ref, vw_ref, gl_ref, ovt_ref, o_ref, out_ref,
               *, n_sel, n_variants):
    i = pl.program_id(1)
    t0 = i * ATT_T
    gates = jax.nn.sigmoid(gl_ref[...].T)

    def gate(hh, branch):
        return gates[3 * hh + branch:3 * hh + branch + 1, :]

    def rel(n):
        return lax.broadcasted_iota(jnp.int32, (n, ATT_T), 1) - lax.broadcasted_iota(jnp.int32, (n, ATT_T), 0)

    w0 = pl.multiple_of(jnp.maximum(t0 - WINDOW, 0), ATT_T)
    dist_w = (t0 - w0) + rel(WIN_KEYS)
    bias_w = jnp.where((dist_w >= 0) & (dist_w < WINDOW), 0.0, MASK_BIAS)

    blk_c = lax.broadcasted_iota(jnp.int32, (N_CMP_PAD, ATT_T), 0)
    d_cmp = t0 + lax.broadcasted_iota(jnp.int32, (N_CMP_PAD, ATT_T), 1) - (blk_c * CMP_STRIDE + (CMP_BLOCK - 1))
    valid_c = d_cmp >= 0
    dm_c = jnp.where(valid_c, d_cmp.astype(F32), MASK_DIST)

    blk = lax.broadcasted_iota(jnp.int32, (n_sel, ATT_T), 0)
    tq = t0 + lax.broadcasted_iota(jnp.int32, (n_sel, ATT_T), 1)
    cur = tq >> SEL_SHIFT
    forced = (blk == 0) | (blk == cur) | (blk == cur - 1)
    eligible = blk * SEL_BLOCK <= tq
    feat_row = lax.broadcasted_iota(jnp.int32, (LANES - BLOCK_LANE - n_sel, ATT_T), 0)

    def lanes(parts):
        return jnp.concatenate(parts, axis=1)

    def tiled(a):
        return lanes([a] * HEAD_PACK)

    bias_w = tiled(bias_w)
    dm_c = tiled(dm_c)
    valid_c = tiled(valid_c)
    pad = jnp.zeros((LANES - HEAD_DIM, ATT_T), BF16)

    def gate_row(heads, branch):
        return lanes([gate(hh, branch) for hh in heads])

    packs = []
    for g in range(NSA_KV_HEADS):
        group = slice(g * LANES, (g + 1) * LANES)
        group_packs = [[g * NSA_GROUP + k + h for h in range(HEAD_PACK)] for k in range(0, NSA_GROUP, HEAD_PACK)]
        slopes = [[2.0 ** -(hh + 1) for hh in heads] for heads in group_packs]
        qs = [[q_ref[hh * HEAD_DIM:(hh + 1) * HEAD_DIM, :] * (HEAD_DIM ** -0.5) for hh in heads]
              for heads in group_packs]

        kc = kc_ref[:, group]
        vc_t = vc_ref[g * HEAD_DIM:(g + 1) * HEAD_DIM, :]
        psum = jnp.zeros((N_CMP_PAD, ATT_T), F32)
        for pk, heads in enumerate(group_packs):
            slope_row = lanes([jnp.full((1, ATT_T), sl, F32) for sl in slopes[pk]])
            sc = _dot(kc, lanes([jnp.concatenate([q, pad], axis=0) for q in qs[pk]])) - slope_row * dm_c
            e = jnp.exp(sc - jnp.max(sc, axis=0, keepdims=True))
            p = e * (1.0 / jnp.sum(e, axis=0, keepdims=True))
            p = jnp.where(valid_c, p, 0.0)
            for h in range(HEAD_PACK):
                psum = psum + p[:, h * ATT_T:(h + 1) * ATT_T]
            out_ref[heads[0] // HEAD_PACK] = gate_row(heads, 0) * _dot(vc_t, p.astype(BF16))

        p_hi = psum.astype(BF16)
        p_lo = (psum - p_hi.astype(F32)).astype(BF16)
        imp = _dot(ovt_ref[...], p_hi) + _dot(ovt_ref[...], p_lo)
        imp = jnp.where(eligible, imp + jnp.where(forced, FORCE_BONUS, 0.0), -1.0)
        rank = jnp.zeros((n_sel, ATT_T), F32)
        for j in range(n_sel):
            cj = imp[j:j + 1, :]
            tie = jnp.where(blk > j, 1.0, 0.0)
            rank = rank + jnp.where(cj > imp, 1.0, jnp.where(cj == imp, tie, 0.0))
        drop = jnp.where(rank < min(SEL_TOPK, n_sel), 0.0, MASK_BIAS).astype(BF16)

        for pk, heads in enumerate(group_packs):
            cols = []
            for q, sl in zip(qs[pk], slopes[pk]):
                alibi = jnp.where(feat_row == ALIBI_LANE - BLOCK_LANE - n_sel, sl * SEL_BLOCK,
                                  jnp.where(feat_row == ALIBI_LANE + 1 - BLOCK_LANE - n_sel, sl, 0.0))
                cols.append(jnp.concatenate([q, drop, alibi.astype(BF16)], axis=0))
            packs.append((g, heads, lanes(cols)))

    def attend(branch, keys_of, values_of, masked):
        def finish(pack, sc):
            g, heads, _ = pack
            out_ref[heads[0] // HEAD_PACK] += gate_row(heads, branch) * _softmax_pv_t(masked(sc), values_of(g))

        pending = None
        for pack in packs:
            sc = _dot(keys_of(pack[0]), pack[2])
            if pending is not None:
                finish(*pending)
            pending = (pack, sc)
        finish(*pending)

    def selected(n_keys):
        causal = tiled(jnp.where(rel(DIAG_KEYS) + (t0 - (n_keys - DIAG_KEYS)) >= 0, 0.0, MASK_BIAS))

        def masked(sc):
            tail = sc[n_keys - DIAG_KEYS:, :] + causal
            return tail if n_keys == DIAG_KEYS else jnp.concatenate([sc[:n_keys - DIAG_KEYS, :], tail], axis=0)

        attend(1, lambda g: ks_ref[0:n_keys, g * LANES:(g + 1) * LANES],
               lambda g: vs_ref[g * V_ROWS:(g + 1) * V_ROWS, 0:n_keys], masked)

    for v in range(n_variants):
        pl.when(lax.div(i, TILES_PER_DIAG) == v)(functools.partial(selected, (v + 1) * DIAG_KEYS))

    attend(2, lambda g: kw_ref[pl.ds(w0, WIN_KEYS), g * LANES:(g + 1) * LANES],
           lambda g: vw_ref[g * V_ROWS:(g + 1) * V_ROWS, pl.ds(w0, WIN_KEYS)], lambda sc: sc + bias_w)

    o_t = [out_ref[hh // HEAD_PACK][:, (hh % HEAD_PACK) * ATT_T:(hh % HEAD_PACK + 1) * ATT_T]
           for hh in range(NSA_HEADS)]
    o_ref[...] = jnp.concatenate(o_t, axis=0).T.astype(o_ref.dtype)


def _attention(q_t, kcb, vcb_t, ks, vs_t, kw, vw_t, gldt, ovt, b, s):
    nq = s // ATT_T
    n_sel = s // SEL_BLOCK
    assert DIAG_KEYS % ATT_T == 0 and s % DIAG_KEYS == 0 and WINDOW % ATT_T == 0
    keys = pl.BlockSpec((s, KV_PAD), lambda r, i: (r, 0))
    vals_t = pl.BlockSpec((NSA_KV_HEADS * V_ROWS, s), lambda r, i: (0, r))
    return pl.pallas_call(
        functools.partial(_attn_body, n_sel=n_sel, n_variants=s // DIAG_KEYS),
        grid=(b, nq),
        in_specs=[
            pl.BlockSpec((NSA_HEADS * HEAD_DIM, ATT_T), lambda r, i: (0, r * nq + i)),
            pl.BlockSpec((None, N_CMP_PAD, KV_PAD), lambda r, i: (r, 0, 0)),
            pl.BlockSpec((None, NSA_KV_HEADS * HEAD_DIM, N_CMP_PAD), lambda r, i: (r, 0, 0)),
            keys, vals_t, keys, vals_t,
            pl.BlockSpec((ATT_T, LANES), lambda r, i: (r * nq + i, 0)),
            pl.BlockSpec(ovt.shape, lambda r, i: (0, 0)),
        ],
        out_specs=pl.BlockSpec((ATT_T, NSA_HEADS * HEAD_DIM), lambda r, i: (r * nq + i, 0)),
        out_shape=jax.ShapeDtypeStruct((b * s, NSA_HEADS * HEAD_DIM), BF16),
        scratch_shapes=[pltpu.VMEM((NSA_HEADS // HEAD_PACK, HEAD_DIM, HEAD_PACK * ATT_T), F32)],
        compiler_params=_params("parallel", "parallel"),
        name="nsa_attention",
    )(q_t, kcb, vcb_t, ks, vs_t, kw, vw_t, gldt, ovt)


CONV_PAD = 8
DT_LANE = 3 * NSA_HEADS


def _split_dot(a, b):
    b0 = b.astype(BF16)
    r1 = b - b0.astype(F32)
    b1 = r1.astype(BF16)
    b2 = (r1 - b1.astype(F32)).astype(BF16)
    return _dot(a, b0) + _dot(a, b1) + _dot(a, b2)


def _split_dot_rhs(a, b):
    a0 = a.astype(BF16)
    r1 = a - a0.astype(F32)
    a1 = r1.astype(BF16)
    a2 = (r1 - a1.astype(F32)).astype(BF16)
    return _dot(a0, b) + _dot(a1, b) + _dot(a2, b)


def _softplus(x):
    return jnp.maximum(x, 0.0) + jnp.log1p(jnp.exp(-jnp.abs(x)))


def _ssd_body(xbc_ref, z_ref, gldt_ref, dte_ref, cw_ref, cb_ref, dtb_e_ref, alog_e_ref, dtb_c_ref, alog_c_ref,
              d_ref, nw_ref, o_ref, state_ref, xe_ref):
    c = pl.program_id(1)
    L = SSM_CHUNK

    @pl.when(c == 0)
    def _():
        state_ref[...] = jnp.zeros(state_ref.shape, F32)
        xe_ref[0:CONV_PAD, :] = jnp.zeros((CONV_PAD, SSM_CONV_DIM), F32)

    x_raw = xbc_ref[...]
    xe_ref[CONV_PAD:CONV_PAD + L, :] = x_raw
    acc = jnp.zeros((L, SSM_CONV_DIM), F32)
    for k in range(SSM_CONV):
        lo = CONV_PAD - (SSM_CONV - 1) + k
        acc = acc + xe_ref[lo:lo + L, :] * cw_ref[k:k + 1, :]
    xe_ref[0:CONV_PAD, :] = x_raw[L - CONV_PAD:, :]
    xa = _silu(acc + cb_ref[...])
    xs = xa[:, :SSM_D_INNER]
    bm = xa[:, SSM_D_INNER:SSM_D_INNER + SSM_BC_DIM]
    cm = xa[:, SSM_D_INNER + SSM_BC_DIM:]

    row = lax.broadcasted_iota(jnp.int32, (L, L), 0)
    col = lax.broadcasted_iota(jnp.int32, (L, L), 1)
    causal = col <= row
    tri = jnp.where(causal, 1.0, 0.0).astype(BF16)
    tri_t = jnp.where(row <= col, 1.0, 0.0).astype(BF16)

    dt = _softplus(dte_ref[...] + dtb_e_ref[...])
    a = dt * (-jnp.exp(alog_e_ref[...]))
    a_cs = _split_dot(tri, a)
    a_last = a_cs[L - 1:L, :]
    dt_t = _softplus(gldt_ref[...].T[DT_LANE:DT_LANE + SSM_HEADS, :] + dtb_c_ref[...])
    a_cs_t = _split_dot_rhs(dt_t * (-jnp.exp(alog_c_ref[...])), tri_t)

    xdt = xs * dt
    xw = (xdt * jnp.exp(a_last - a_cs)).astype(BF16)
    exp_cs = jnp.exp(a_cs)
    lane = lax.broadcasted_iota(jnp.int32, (L, LANES), 1)
    rpg = SSM_HEADS // SSM_GROUPS
    gw = rpg * SSM_HEAD_DIM
    ys = []
    for g in range(SSM_GROUPS):
        bm_g = bm[:, g * SSM_STATE:(g + 1) * SSM_STATE]
        cm_g = cm[:, g * SSM_STATE:(g + 1) * SSM_STATE].astype(BF16)
        cb = _dot_nt(cm_g, bm_g.astype(BF16))
        prev = state_ref[:, g * gw:(g + 1) * gw]
        y_off = _dot(cm_g, prev.astype(BF16)) * exp_cs[:, g * gw:(g + 1) * gw]
        new_state = _dot(bm_g.T.astype(BF16), xw[:, g * gw:(g + 1) * gw])
        state_ref[:, g * gw:(g + 1) * gw] = prev * jnp.exp(a_last[:, g * gw:(g + 1) * gw]) + new_state
        for pair in range(rpg // 2):
            h0 = g * rpg + 2 * pair
            x_pair = xdt[:, h0 * SSM_HEAD_DIM:(h0 + 2) * SSM_HEAD_DIM].astype(BF16)
            halves = []
            for h in (h0, h0 + 1):
                diff = a_cs[:, h * SSM_HEAD_DIM:h * SSM_HEAD_DIM + 1] - a_cs_t[h:h + 1, :]
                decay = jnp.exp(jnp.where(causal, diff, NEG_INF))
                halves.append(_dot((cb * decay).astype(BF16), x_pair))
            ys.append(jnp.where(lane < SSM_HEAD_DIM, halves[0], halves[1])
                      + y_off[:, 2 * pair * SSM_HEAD_DIM:(2 * pair + 2) * SSM_HEAD_DIM])
    y = jnp.concatenate(ys, axis=-1) + xs * d_ref[...]
    y = y * _silu(z_ref[...])
    outs = []
    for g in range(SSM_GROUPS):
        yg = y[:, g * gw:(g + 1) * gw]
        outs.append(_rms(yg, nw_ref[:, g * gw:(g + 1) * gw]))
    o_ref[...] = jnp.concatenate(outs, axis=-1).astype(o_ref.dtype)


def _ssd(xbc, z, gldt, dte, conv_w, conv_b, dtb_e, alog_e, dtb_c, alog_c, d_e, norm_w, b, s):
    nc = s // SSM_CHUNK
    rows = lambda n: pl.BlockSpec((SSM_CHUNK, n), lambda r, c: (r * nc + c, 0))
    full = lambda arr: pl.BlockSpec(arr.shape, lambda r, c: (0,) * arr.ndim)
    consts = (conv_w, conv_b, dtb_e, alog_e, dtb_c, alog_c, d_e, norm_w)
    return pl.pallas_call(
        _ssd_body,
        grid=(b, nc),
        in_specs=[rows(SSM_CONV_DIM), rows(SSM_D_INNER), rows(LANES), rows(SSM_D_INNER)] + [full(a) for a in consts],
        out_specs=rows(SSM_D_INNER),
        out_shape=jax.ShapeDtypeStruct((b * s, SSM_D_INNER), BF16),
        scratch_shapes=[
            pltpu.VMEM((SSM_STATE, SSM_D_INNER), F32),
            pltpu.VMEM((CONV_PAD + SSM_CHUNK, SSM_CONV_DIM), F32),
        ],
        compiler_params=_params("parallel", "arbitrary"),
        name="ssd",
    )(xbc, z, gldt, dte, *consts)


def _outproj_body(h_ref, oa_ref, os_ref, g_ref, w_ref, o_ref):
    na = oa_ref.shape[-1]
    y = _dot(oa_ref[...], w_ref[:na, :]) + _dot(os_ref[...], w_ref[na:, :])
    o_ref[...] = h_ref[...] + _rms(y, g_ref[1:2, :])


def _outproj(h, oa, os_, gains, w, layer):
    t = h.shape[0]
    rows = lambda n: pl.BlockSpec((PROJ_TM, n), lambda r: (r, 0))
    return pl.pallas_call(
        _outproj_body,
        grid=(t // PROJ_TM,),
        in_specs=[
            rows(D_MODEL), rows(oa.shape[1]), rows(os_.shape[1]),
            pl.BlockSpec((None, 4, D_MODEL), lambda r: (layer, 0, 0)),
            pl.BlockSpec(w.shape, lambda r: (0, 0)),
        ],
        out_specs=rows(D_MODEL),
        out_shape=jax.ShapeDtypeStruct((t, D_MODEL), F32),
        compiler_params=_params("parallel"),
        name="outproj",
    )(h, oa, os_, gains, w)


def _even_mixer(h, gains, layer, w_in, w_out, pe, w1, w2, conv_w, conv_b, dt_bias, a_log, d_skip, norm_w, b, s):
    w_rows, w_cols = _prep_w_in(w_in)
    kvc, ks, kw, z, xbc, gldt, dte, q_t, vs_t, vw_t = _inproj(h, gains, w_rows, w_cols, _position_features(s), layer)
    kcb, vcb_t = _compress(kvc, *_prep_cmp(pe, w1, w2), b, s)
    o_attn = _attention(q_t, kcb, vcb_t, ks, vs_t, kw, vw_t, gldt, _overlap_t(s), b, s)
    rep = lambda v: jnp.repeat(v, SSM_HEAD_DIM)[None, :]
    o_ssm = _ssd(xbc, z, gldt, dte, conv_w, conv_b[None, :], rep(dt_bias), rep(a_log), dt_bias[:, None],
                 a_log[:, None], rep(d_skip), norm_w[None, :], b, s)
    return _outproj(h, o_attn, o_ssm, gains, w_out.astype(BF16), layer)


def kernel(x, norm_gains, ffn_w_gate, ffn_w_up, ffn_w_down, ev_w_in, ev_w_out, nsa_cmp_pe, nsa_cmp_w1, nsa_cmp_w2,
           ssm_conv_w, ssm_conv_b, ssm_dt_bias, ssm_a_log, ssm_d, ssm_norm_w, od_w_in, od_ln_w, od_ln_b, od_w_s,
           od_b_s, od_w_out):
    b, s, d = x.shape
    depth = norm_gains.shape[0]
    h = x.reshape(b * s, d)
    wg = ffn_w_gate.astype(BF16)
    wu = ffn_w_up.astype(BF16)
    wd = ffn_w_down.astype(BF16)
    od_in = od_w_in.astype(BF16)
    od_out = od_w_out.astype(BF16)
    od_lnw = od_ln_w[:, None, :]
    od_lnb = od_ln_b[:, None, :]
    od_bst = jnp.swapaxes(od_b_s, 1, 2)
    for layer in range(depth):
        i = layer // 2
        if layer % 2 == 0:
            h = _even_mixer(h, norm_gains, layer, ev_w_in[i], ev_w_out[i], nsa_cmp_pe[i], nsa_cmp_w1[i],
                            nsa_cmp_w2[i], ssm_conv_w[i], ssm_conv_b[i], ssm_dt_bias[i], ssm_a_log[i], ssm_d[i],
                            ssm_norm_w[i], b, s)
        else:
            h = _gmlp(h, norm_gains, od_in, od_lnw, od_lnb, od_w_s, od_bst, od_out, layer, i)
        h = _ffn(h, norm_gains, wg, wu, wd, layer)
    return h.reshape(b, s, d)
```

```python
import functools

import numpy as np
import jax
import jax.numpy as jnp
from jax import lax
from jax.experimental import pallas as pl
from jax.experimental.pallas import tpu as pltpu

F32 = jnp.float32
BF16 = jnp.bfloat16

D_MODEL = 1024
RMS_EPS = 1e-6
LN_EPS = 1e-5
NEG_INF = -1e30

NSA_HEADS = 8
NSA_KV_HEADS = 2
NSA_GROUP = NSA_HEADS // NSA_KV_HEADS
HEAD_DIM = 64
CMP_BLOCK = 32
CMP_STRIDE = 16
CMP_HIDDEN = 256
SEL_BLOCK = 64
SEL_TOPK = 16
WINDOW = 512
FORCE_BONUS = 1e4

SSM_HEADS = 8
SSM_HEAD_DIM = 64
SSM_D_INNER = SSM_HEADS * SSM_HEAD_DIM
SSM_GROUPS = 2
SSM_STATE = 128
SSM_CONV = 4
SSM_CHUNK = 128
SSM_BC_DIM = SSM_GROUPS * SSM_STATE
SSM_CONV_DIM = SSM_D_INNER + 2 * SSM_BC_DIM

GMLP_WIDTH = 2 * D_MODEL
GMLP_GROUPS = 8
GMLP_GROUP_DIM = GMLP_WIDTH // GMLP_GROUPS
GMLP_CHUNK = 128

LANES = 128
VMEM_LIMIT = 56 * 1024 * 1024


def _params(*sem):
    return pltpu.CompilerParams(dimension_semantics=sem, vmem_limit_bytes=VMEM_LIMIT)


def _rms(x, g):
    return x * lax.rsqrt(jnp.mean(x * x, axis=-1, keepdims=True) + RMS_EPS) * g


def _silu(x):
    return x * jax.nn.sigmoid(x)


def _dot(a, b):
    return jnp.dot(a, b, preferred_element_type=F32)


def _dot_nt(a, b):
    return lax.dot_general(a, b, (((1,), (1,)), ((), ())), preferred_element_type=F32)


def _resident(block_shape, index_map):
    return pl.BlockSpec(block_shape, index_map, pipeline_mode=pl.Buffered(1))


FFN_TM = 512
FFN_CHUNK = 256


def _ffn_body(h_ref, g_ref, wg_ref, wu_ref, wd_ref, *rest):
    o_ref = rest[-1]
    x = h_ref[...]
    if len(rest) > 1:
        oa_ref, os_ref, wo_ref = rest[:3]
        na = oa_ref.shape[-1]
        x = x + _rms(_dot(oa_ref[...], wo_ref[:na, :]) + _dot(os_ref[...], wo_ref[na:, :]), g_ref[1:2, :])
    xn = _rms(x, g_ref[2:3, :]).astype(BF16)
    acc = jnp.zeros(x.shape, F32)
    for c in range(wg_ref.shape[1] // FFN_CHUNK):
        cols = slice(c * FFN_CHUNK, (c + 1) * FFN_CHUNK)
        a = _dot(xn, wg_ref[:, cols])
        b = _dot(xn, wu_ref[:, cols])
        acc = acc + _dot((_silu(a) * b).astype(BF16), wd_ref[cols, :])
    o_ref[...] = x + _rms(acc, g_ref[3:4, :])


def _ffn(h, gains, wg, wu, wd, layer, mixer_out=None):
    t = h.shape[0]
    hidden = wg.shape[2]
    rows = lambda n: pl.BlockSpec((FFN_TM, n), lambda i: (i, 0))
    in_specs = [
        rows(D_MODEL),
        pl.BlockSpec((None, 4, D_MODEL), lambda i: (layer, 0, 0)),
        _resident((None, D_MODEL, hidden), lambda i: (layer, 0, 0)),
        _resident((None, D_MODEL, hidden), lambda i: (layer, 0, 0)),
        _resident((None, hidden, D_MODEL), lambda i: (layer, 0, 0)),
    ]
    args = [h, gains, wg, wu, wd]
    if mixer_out is not None:
        oa, os_, w_out = mixer_out
        in_specs += [rows(oa.shape[1]), rows(os_.shape[1]), _resident(w_out.shape, lambda i: (0, 0))]
        args += [oa, os_, w_out]
    return pl.pallas_call(
        _ffn_body,
        grid=(t // FFN_TM,),
        in_specs=in_specs,
        out_specs=rows(D_MODEL),
        out_shape=jax.ShapeDtypeStruct((t, D_MODEL), F32),
        compiler_params=_params("parallel"),
        name="ffn",
    )(*args)


GMLP_TM = 512


def _gelu(x):
    return 0.5 * x * (1.0 + lax.erf(x * np.float32(1.0 / np.sqrt(2.0))))


GMLP_COLS = 1024


def _gmlp_body(h_ref, g_ref, win_ref, lnw_ref, lnb_ref, ws_ref, bs_ref, wout_ref, o_ref):
    x = h_ref[...]
    xn = _rms(x, g_ref[0:1, :]).astype(BF16)
    v = _gelu(_dot(xn, win_ref[:, GMLP_WIDTH:]))
    mu = jnp.mean(v, axis=-1, keepdims=True)
    vc = v - mu
    var = jnp.mean(vc * vc, axis=-1, keepdims=True)
    v = (vc * lax.rsqrt(var + LN_EPS) * lnw_ref[...] + lnb_ref[...]).astype(BF16)
    row = lax.broadcasted_iota(jnp.int32, (GMLP_CHUNK, GMLP_CHUNK), 0)
    col = lax.broadcasted_iota(jnp.int32, (GMLP_CHUNK, GMLP_CHUNK), 1)
    causal = col <= row
    y = jnp.zeros(x.shape, F32)
    for c0 in range(0, GMLP_WIDTH, GMLP_COLS):
        u = _gelu(_dot(xn, win_ref[:, c0:c0 + GMLP_COLS]))
        mixed = []
        for g in range(c0 // GMLP_GROUP_DIM, (c0 + GMLP_COLS) // GMLP_GROUP_DIM):
            ws = jnp.where(causal, ws_ref[g], 0.0).astype(BF16)
            bias = bs_ref[:, g:g + 1]
            cols = slice(g * GMLP_GROUP_DIM, (g + 1) * GMLP_GROUP_DIM)
            mixed.append(jnp.concatenate(
                [_dot(ws, v[c * GMLP_CHUNK:(c + 1) * GMLP_CHUNK, cols]) + bias for c in range(GMLP_TM // GMLP_CHUNK)],
                axis=0))
        gate = (u * jnp.concatenate(mixed, axis=1)).astype(BF16)
        y = y + _dot(gate, wout_ref[c0:c0 + GMLP_COLS, :])
    o_ref[...] = x + _rms(y, g_ref[1:2, :])


def _gmlp(h, gains, w_in, ln_w, ln_b, w_s, b_s_t, w_out, layer, i):
    t = h.shape[0]
    const2 = lambda r: (i, 0, 0)
    return pl.pallas_call(
        _gmlp_body,
        grid=(t // GMLP_TM,),
        in_specs=[
            pl.BlockSpec((GMLP_TM, D_MODEL), lambda r: (r, 0)),
            pl.BlockSpec((None, 4, D_MODEL), lambda r: (layer, 0, 0)),
            _resident((None, D_MODEL, 2 * GMLP_WIDTH), const2),
            pl.BlockSpec((None, 1, GMLP_WIDTH), const2),
            pl.BlockSpec((None, 1, GMLP_WIDTH), const2),
            pl.BlockSpec((None, GMLP_GROUPS, GMLP_CHUNK, GMLP_CHUNK), lambda r: (i, 0, 0, 0)),
            pl.BlockSpec((None, GMLP_CHUNK, GMLP_GROUPS), const2),
            _resident((None, GMLP_WIDTH, D_MODEL), const2),
        ],
        out_specs=pl.BlockSpec((GMLP_TM, D_MODEL), lambda r: (r, 0)),
        out_shape=jax.ShapeDtypeStruct((t, D_MODEL), F32),
        compiler_params=_params("parallel"),
        name="gmlp",
    )(h, gains, w_in, ln_w, ln_b, w_s, b_s_t, w_out)


PROJ_TM = 512
KV_PAD = NSA_KV_HEADS * LANES
BLOCK_LANE = HEAD_DIM
ALIBI_LANE = 96
V_ROWS = HEAD_DIM + 16
ONES_ROW = HEAD_DIM
ROW_SEGS = ((2 * LANES, F32, False),
            (KV_PAD, BF16, True),
            (KV_PAD, BF16, True),
            (SSM_D_INNER, F32, False),
            (SSM_CONV_DIM, F32, False),
            (LANES, F32, False),
            (SSM_D_INNER, F32, False))
COL_SEGS = ((NSA_HEADS * HEAD_DIM, False),
            (NSA_KV_HEADS * V_ROWS, True),
            (NSA_KV_HEADS * V_ROWS, True))
ROW_WIDTH = sum(seg[0] for seg in ROW_SEGS)
COL_WIDTH = sum(seg[0] for seg in COL_SEGS)
FEAT_WIDTH = sum(seg[0] for seg in ROW_SEGS if seg[2])


def _prep_w_in(w):
    nq = NSA_HEADS * HEAD_DIM
    nkv = NSA_KV_HEADS * HEAD_DIM

    def pad_groups(a, width):
        a = a.reshape(D_MODEL, NSA_KV_HEADS, HEAD_DIM)
        return jnp.pad(a, ((0, 0), (0, 0), (0, width - HEAD_DIM))).reshape(D_MODEL, NSA_KV_HEADS * width)

    o = nq
    kc, vc, ks, vs, kw, vw = [w[:, o + j * nkv:o + (j + 1) * nkv] for j in range(6)]
    o += 6 * nkv
    gl = w[:, o:o + 3 * NSA_HEADS]
    o += 3 * NSA_HEADS
    z = w[:, o:o + SSM_D_INNER]
    o += SSM_D_INNER
    xbc = w[:, o:o + SSM_CONV_DIM]
    o += SSM_CONV_DIM
    dt = w[:, o:o + SSM_HEADS]
    gldt = jnp.concatenate([gl, dt, jnp.zeros((D_MODEL, LANES - 3 * NSA_HEADS - SSM_HEADS), w.dtype)], axis=1)
    dte = jnp.repeat(dt, SSM_HEAD_DIM, axis=1)
    rows = jnp.concatenate([kc, vc, pad_groups(ks, LANES), pad_groups(kw, LANES), z, xbc, gldt, dte], axis=1)
    cols = jnp.concatenate([w[:, :nq], pad_groups(vs, V_ROWS), pad_groups(vw, V_ROWS)], axis=1).T
    assert rows.shape[1] == ROW_WIDTH and cols.shape[0] == COL_WIDTH
    return rows.astype(BF16), cols.astype(BF16)


def _position_features(s):
    n_sel = s // SEL_BLOCK
    assert BLOCK_LANE + n_sel <= ALIBI_LANE and ALIBI_LANE + 2 <= LANES
    pos = np.arange(s)
    alibi = np.zeros((s, LANES), np.float32)
    alibi[:, ALIBI_LANE] = pos // SEL_BLOCK
    alibi[:, ALIBI_LANE + 1] = pos % SEL_BLOCK
    k_sel = alibi.copy()
    k_sel[pos, BLOCK_LANE + pos // SEL_BLOCK] = 1.0
    feats = [np.tile(a, (1, NSA_KV_HEADS)) for a in (k_sel, alibi)]
    return jnp.asarray(np.concatenate(feats, axis=1), BF16)


def _inproj_body(h_ref, g_ref, wr_ref, wc_ref, feat_ref, *out_refs):
    xn = _rms(h_ref[...], g_ref[0:1, :]).astype(BF16)
    row_refs = out_refs[:len(ROW_SEGS)]
    col_refs = out_refs[len(ROW_SEGS):]
    off = 0
    feat_off = 0
    for ref, (n, dtype, has_feat) in zip(row_refs, ROW_SEGS):
        val = _dot(xn, wr_ref[:, off:off + n]).astype(dtype)
        if has_feat:
            val = val + feat_ref[:, feat_off:feat_off + n]
            feat_off += n
        ref[...] = val
        off += n
    off = 0
    for ref, (n, has_ones) in zip(col_refs, COL_SEGS):
        val = _dot_nt(wc_ref[off:off + n, :], xn)
        if has_ones:
            row = lax.broadcasted_iota(jnp.int32, val.shape, 0)
            is_ones = row == ONES_ROW
            for g in range(1, NSA_KV_HEADS):
                is_ones = is_ones | (row == g * V_ROWS + ONES_ROW)
            val = jnp.where(is_ones, 1.0, val)
        ref[...] = val.astype(BF16)
        off += n


def _inproj(h, gains, w_rows, w_cols, feats, layer):
    t = h.shape[0]
    tiles_per_seq = feats.shape[0] // PROJ_TM
    return pl.pallas_call(
        _inproj_body,
        grid=(t // PROJ_TM,),
        in_specs=[
            pl.BlockSpec((PROJ_TM, D_MODEL), lambda r: (r, 0)),
            pl.BlockSpec((None, 4, D_MODEL), lambda r: (layer, 0, 0)),
            _resident((D_MODEL, ROW_WIDTH), lambda r: (0, 0)),
            _resident((COL_WIDTH, D_MODEL), lambda r: (0, 0)),
            pl.BlockSpec((PROJ_TM, FEAT_WIDTH), lambda r: (lax.rem(r, tiles_per_seq), 0)),
        ],
        out_specs=([pl.BlockSpec((PROJ_TM, seg[0]), lambda r: (r, 0)) for seg in ROW_SEGS]
                   + [pl.BlockSpec((seg[0], PROJ_TM), lambda r: (0, r)) for seg in COL_SEGS]),
        out_shape=([jax.ShapeDtypeStruct((t, seg[0]), seg[1]) for seg in ROW_SEGS]
                   + [jax.ShapeDtypeStruct((seg[0], t), BF16) for seg in COL_SEGS]),
        compiler_params=_params("parallel"),
        name="inproj",
    )(h, gains, w_rows, w_cols, feats)


N_CMP_PAD = 128
CMP_HALF = CMP_BLOCK // CMP_STRIDE


def _prep_cmp(pe, w1, w2):
    pe2 = jnp.concatenate([pe, pe], axis=-1)
    w1r = w1.reshape(2, CMP_HALF, CMP_STRIDE, HEAD_DIM, CMP_HIDDEN)
    w1d = jnp.concatenate([w1r, w1r], axis=3).reshape(2, CMP_HALF, CMP_STRIDE * LANES, CMP_HIDDEN)
    w2k = jnp.stack([jnp.pad(w2[0], ((0, 0), (g * LANES, KV_PAD - g * LANES - HEAD_DIM)))
                     for g in range(NSA_KV_HEADS)])
    return pe2, w1d.astype(BF16), w2k.astype(BF16), w2[1].T.astype(BF16)


def _compress_body(kc_ref, vc_ref, pe_ref, w1_ref, w2k_ref, w2v_ref, ok_ref, ov_ref):
    lane = lax.broadcasted_iota(jnp.int32, (N_CMP_PAD, LANES), 1)
    out_k = jnp.zeros((N_CMP_PAD, KV_PAD), F32)
    for kv in range(2):
        src = (kc_ref, vc_ref)[kv]
        pieces = [src[pl.ds(l, N_CMP_PAD, stride=CMP_STRIDE), :] for l in range(CMP_STRIDE)]
        for g in range(NSA_KV_HEADS):
            in_group = (lane >= g * HEAD_DIM) & (lane < (g + 1) * HEAD_DIM)
            pre = None
            for half in range(CMP_HALF):
                xs = [jnp.where(in_group, pieces[l] + pe_ref[kv, half * CMP_STRIDE + l:half * CMP_STRIDE + l + 1, :], 0.0)
                      for l in range(CMP_STRIDE)]
                xcat = jnp.concatenate(xs, axis=-1).astype(BF16)
                part = _dot(xcat, w1_ref[kv, half])
                if half == 1:
                    part = pltpu.roll(part, N_CMP_PAD - 1, 0)
                pre = part if pre is None else pre + part
            hid = _silu(pre).astype(BF16)
            if kv == 0:
                out_k = out_k + _dot(hid, w2k_ref[g])
            else:
                ov_ref[g * HEAD_DIM:(g + 1) * HEAD_DIM, :] = _dot_nt(w2v_ref[...], hid).astype(BF16)
    ok_ref[...] = out_k.astype(BF16)


def _compress(kvc, pe2, w1d, w2k, w2v, b, s):
    return pl.pallas_call(
        _compress_body,
        grid=(b,),
        in_specs=[
            pl.BlockSpec((s, LANES), lambda r: (r, 0)),
            pl.BlockSpec((s, LANES), lambda r: (r, 1)),
            pl.BlockSpec(pe2.shape, lambda r: (0, 0, 0)),
            pl.BlockSpec(w1d.shape, lambda r: (0, 0, 0, 0)),
            pl.BlockSpec(w2k.shape, lambda r: (0, 0, 0)),
            pl.BlockSpec(w2v.shape, lambda r: (0, 0)),
        ],
        out_specs=[pl.BlockSpec((None, N_CMP_PAD, KV_PAD), lambda r: (r, 0, 0)),
                   pl.BlockSpec((None, NSA_KV_HEADS * HEAD_DIM, N_CMP_PAD), lambda r: (r, 0, 0))],
        out_shape=[jax.ShapeDtypeStruct((b, N_CMP_PAD, KV_PAD), BF16),
                   jax.ShapeDtypeStruct((b, NSA_KV_HEADS * HEAD_DIM, N_CMP_PAD), BF16)],
        compiler_params=_params("parallel"),
        name="compress",
    )(kvc, kvc, pe2, w1d, w2k, w2v)


ATT_T = 256
SEL_SHIFT = SEL_BLOCK.bit_length() - 1
MASK_DIST = 2.0 ** 100
MASK_BIAS = -(2.0 ** 100)
WIN_KEYS = WINDOW + ATT_T
DIAG_KEYS = 512
TILES_PER_DIAG = DIAG_KEYS // ATT_T
HEAD_PACK = 4


def _overlap_t(s):
    n_cmp = (s - CMP_BLOCK) // CMP_STRIDE + 1
    n_sel = s // SEL_BLOCK
    assert n_cmp <= N_CMP_PAD
    cs = np.arange(n_cmp)[None, :] * CMP_STRIDE
    ss = np.arange(n_sel)[:, None] * SEL_BLOCK
    ov = np.clip(np.minimum(cs + CMP_BLOCK, ss + SEL_BLOCK) - np.maximum(cs, ss), 0, None) / CMP_BLOCK
    return jnp.asarray(np.pad(ov, ((0, 0), (0, N_CMP_PAD - n_cmp))), BF16)


def _softmax_pv_t(sc, v_t):
    p = jnp.exp(sc - jnp.max(sc, axis=0, keepdims=True))
    o = _dot(v_t, p.astype(BF16))
    return o[0:HEAD_DIM, :] * (1.0 / o[ONES_ROW:ONES_ROW + 1, :])


def _attn_body(q_ref, kc_ref, vc_ref, ks_ref, vs_ref, kw_ref, vw_ref, gl_ref, ovt_ref, o_ref, out_ref,
               *, n_sel, n_variants):
    i = pl.program_id(1)
    t0 = i * ATT_T
    gates = jax.nn.sigmoid(gl_ref[...].T)

    def gate(hh, branch):
        return gates[3 * hh + branch:3 * hh + branch + 1, :]

    def rel(n):
        return lax.broadcasted_iota(jnp.int32, (n, ATT_T), 1) - lax.broadcasted_iota(jnp.int32, (n, ATT_T), 0)

    w0 = pl.multiple_of(jnp.maximum(t0 - WINDOW, 0), ATT_T)
    dist_w = (t0 - w0) + rel(WIN_KEYS)
    bias_w = jnp.where((dist_w >= 0) & (dist_w < WINDOW), 0.0, MASK_BIAS)

    blk_c = lax.broadcasted_iota(jnp.int32, (N_CMP_PAD, ATT_T), 0)
    d_cmp = t0 + lax.broadcasted_iota(jnp.int32, (N_CMP_PAD, ATT_T), 1) - (blk_c * CMP_STRIDE + (CMP_BLOCK - 1))
    valid_c = d_cmp >= 0
    dm_c = jnp.where(valid_c, d_cmp.astype(F32), MASK_DIST)

    blk = lax.broadcasted_iota(jnp.int32, (n_sel, ATT_T), 0)
    tq = t0 + lax.broadcasted_iota(jnp.int32, (n_sel, ATT_T), 1)
    cur = tq >> SEL_SHIFT
    forced = (blk == 0) | (blk == cur) | (blk == cur - 1)
    eligible = blk * SEL_BLOCK <= tq
    feat_row = lax.broadcasted_iota(jnp.int32, (LANES - BLOCK_LANE - n_sel, ATT_T), 0)

    def lanes(parts):
        return jnp.concatenate(parts, axis=1)

    def tiled(a):
        return lanes([a] * HEAD_PACK)

    bias_w = tiled(bias_w)
    dm_c = tiled(dm_c)
    valid_c = tiled(valid_c)
    pad = jnp.zeros((LANES - HEAD_DIM, ATT_T), BF16)

    def gate_row(heads, branch):
        return lanes([gate(hh, branch) for hh in heads])

    packs = []
    for g in range(NSA_KV_HEADS):
        group = slice(g * LANES, (g + 1) * LANES)
        group_packs = [[g * NSA_GROUP + k + h for h in range(HEAD_PACK)] for k in range(0, NSA_GROUP, HEAD_PACK)]
        slopes = [[2.0 ** -(hh + 1) for hh in heads] for heads in group_packs]
        qs = [[q_ref[hh * HEAD_DIM:(hh + 1) * HEAD_DIM, :] * (HEAD_DIM ** -0.5) for hh in heads]
              for heads in group_packs]

        kc = kc_ref[:, group]
        vc_t = vc_ref[g * HEAD_DIM:(g + 1) * HEAD_DIM, :]
        psum = jnp.zeros((N_CMP_PAD, ATT_T), F32)
        for pk, heads in enumerate(group_packs):
            slope_row = lanes([jnp.full((1, ATT_T), sl, F32) for sl in slopes[pk]])
            sc = _dot(kc, lanes([jnp.concatenate([q, pad], axis=0) for q in qs[pk]])) - slope_row * dm_c
            e = jnp.exp(sc - jnp.max(sc, axis=0, keepdims=True))
            p = e * (1.0 / jnp.sum(e, axis=0, keepdims=True))
            p = jnp.where(valid_c, p, 0.0)
            for h in range(HEAD_PACK):
                psum = psum + p[:, h * ATT_T:(h + 1) * ATT_T]
            out_ref[heads[0] // HEAD_PACK] = gate_row(heads, 0) * _dot(vc_t, p.astype(BF16))

        p_hi = psum.astype(BF16)
        p_lo = (psum - p_hi.astype(F32)).astype(BF16)
        imp = _dot(ovt_ref[...], p_hi) + _dot(ovt_ref[...], p_lo)
        imp = jnp.where(eligible, imp + jnp.where(forced, FORCE_BONUS, 0.0), -1.0)
        rank = jnp.zeros((n_sel, ATT_T), F32)
        for j in range(n_sel):
            cj = imp[j:j + 1, :]
            tie = jnp.where(blk > j, 1.0, 0.0)
            rank = rank + jnp.where(cj > imp, 1.0, jnp.where(cj == imp, tie, 0.0))
        drop = jnp.where(rank < min(SEL_TOPK, n_sel), 0.0, MASK_BIAS).astype(BF16)

        for pk, heads in enumerate(group_packs):
            cols = []
            for q, sl in zip(qs[pk], slopes[pk]):
                alibi = jnp.where(feat_row == ALIBI_LANE - BLOCK_LANE - n_sel, sl * SEL_BLOCK,
                                  jnp.where(feat_row == ALIBI_LANE + 1 - BLOCK_LANE - n_sel, sl, 0.0))
                cols.append(jnp.concatenate([q, drop, alibi.astype(BF16)], axis=0))
            packs.append((g, heads, lanes(cols)))

    def attend(branch, keys_of, values_of, masked):
        def finish(pack, sc):
            g, heads, _ = pack
            out_ref[heads[0] // HEAD_PACK] += gate_row(heads, branch) * _softmax_pv_t(masked(sc), values_of(g))

        pending = None
        for pack in packs:
            sc = _dot(keys_of(pack[0]), pack[2])
            if pending is not None:
                finish(*pending)
            pending = (pack, sc)
        finish(*pending)

    def selected(n_keys):
        causal = tiled(jnp.where(rel(DIAG_KEYS) + (t0 - (n_keys - DIAG_KEYS)) >= 0, 0.0, MASK_BIAS))

        def masked(sc):
            tail = sc[n_keys - DIAG_KEYS:, :] + causal
            return tail if n_keys == DIAG_KEYS else jnp.concatenate([sc[:n_keys - DIAG_KEYS, :], tail], axis=0)

        attend(1, lambda g: ks_ref[0:n_keys, g * LANES:(g + 1) * LANES],
               lambda g: vs_ref[g * V_ROWS:(g + 1) * V_ROWS, 0:n_keys], masked)

    for v in range(n_variants):
        pl.when(lax.div(i, TILES_PER_DIAG) == v)(functools.partial(selected, (v + 1) * DIAG_KEYS))

    attend(2, lambda g: kw_ref[pl.ds(w0, WIN_KEYS), g * LANES:(g + 1) * LANES],
           lambda g: vw_ref[g * V_ROWS:(g + 1) * V_ROWS, pl.ds(w0, WIN_KEYS)], lambda sc: sc + bias_w)

    o_t = [out_ref[hh // HEAD_PACK][:, (hh % HEAD_PACK) * ATT_T:(hh % HEAD_PACK + 1) * ATT_T]
           for hh in range(NSA_HEADS)]
    o_ref[...] = jnp.concatenate(o_t, axis=0).T.astype(o_ref.dtype)


def _attention(q_t, kcb, vcb_t, ks, vs_t, kw, vw_t, gldt, ovt, b, s):
    nq = s // ATT_T
    n_sel = s // SEL_BLOCK
    assert DIAG_KEYS % ATT_T == 0 and s % DIAG_KEYS == 0 and WINDOW % ATT_T == 0
    keys = pl.BlockSpec((s, KV_PAD), lambda r, i: (r, 0))
    vals_t = pl.BlockSpec((NSA_KV_HEADS * V_ROWS, s), lambda r, i: (0, r))
    return pl.pallas_call(
        functools.partial(_attn_body, n_sel=n_sel, n_variants=s // DIAG_KEYS),
        grid=(b, nq),
        in_specs=[
            pl.BlockSpec((NSA_HEADS * HEAD_DIM, ATT_T), lambda r, i: (0, r * nq + i)),
            pl.BlockSpec((None, N_CMP_PAD, KV_PAD), lambda r, i: (r, 0, 0)),
            pl.BlockSpec((None, NSA_KV_HEADS * HEAD_DIM, N_CMP_PAD), lambda r, i: (r, 0, 0)),
            keys, vals_t, keys, vals_t,
            pl.BlockSpec((ATT_T, LANES), lambda r, i: (r * nq + i, 0)),
            pl.BlockSpec(ovt.shape, lambda r, i: (0, 0)),
        ],
        out_specs=pl.BlockSpec((ATT_T, NSA_HEADS * HEAD_DIM), lambda r, i: (r * nq + i, 0)),
        out_shape=jax.ShapeDtypeStruct((b * s, NSA_HEADS * HEAD_DIM), BF16),
        scratch_shapes=[pltpu.VMEM((NSA_HEADS // HEAD_PACK, HEAD_DIM, HEAD_PACK * ATT_T), F32)],
        compiler_params=_params("parallel", "parallel"),
        name="nsa_attention",
    )(q_t, kcb, vcb_t, ks, vs_t, kw, vw_t, gldt, ovt)


CONV_PAD = 8
DT_LANE = 3 * NSA_HEADS


def _split_dot(a, b):
    b0 = b.astype(BF16)
    r1 = b - b0.astype(F32)
    b1 = r1.astype(BF16)
    b2 = (r1 - b1.astype(F32)).astype(BF16)
    return _dot(a, b0) + _dot(a, b1) + _dot(a, b2)


def _split_dot_rhs(a, b):
    a0 = a.astype(BF16)
    r1 = a - a0.astype(F32)
    a1 = r1.astype(BF16)
    a2 = (r1 - a1.astype(F32)).astype(BF16)
    return _dot(a0, b) + _dot(a1, b) + _dot(a2, b)


def _softplus(x):
    return jnp.maximum(x, 0.0) + jnp.log1p(jnp.exp(-jnp.abs(x)))


def _ssd_body(xbc_ref, z_ref, gldt_ref, dte_ref, cw_ref, cb_ref, dtb_e_ref, alog_e_ref, dtb_c_ref, alog_c_ref,
              d_ref, nw_ref, o_ref, state_ref, xe_ref):
    c = pl.program_id(1)
    L = SSM_CHUNK

    @pl.when(c == 0)
    def _():
        state_ref[...] = jnp.zeros(state_ref.shape, F32)
        xe_ref[0:CONV_PAD, :] = jnp.zeros((CONV_PAD, SSM_CONV_DIM), F32)

    x_raw = xbc_ref[...]
    xe_ref[CONV_PAD:CONV_PAD + L, :] = x_raw
    acc = jnp.zeros((L, SSM_CONV_DIM), F32)
    for k in range(SSM_CONV):
        lo = CONV_PAD - (SSM_CONV - 1) + k
        acc = acc + xe_ref[lo:lo + L, :] * cw_ref[k:k + 1, :]
    xe_ref[0:CONV_PAD, :] = x_raw[L - CONV_PAD:, :]
    xa = _silu(acc + cb_ref[...])
    xs = xa[:, :SSM_D_INNER]
    bm = xa[:, SSM_D_INNER:SSM_D_INNER + SSM_BC_DIM]
    cm = xa[:, SSM_D_INNER + SSM_BC_DIM:]

    row = lax.broadcasted_iota(jnp.int32, (L, L), 0)
    col = lax.broadcasted_iota(jnp.int32, (L, L), 1)
    causal = col <= row
    tri = jnp.where(causal, 1.0, 0.0).astype(BF16)
    tri_t = jnp.where(row <= col, 1.0, 0.0).astype(BF16)

    dt = _softplus(dte_ref[...] + dtb_e_ref[...])
    a = dt * (-jnp.exp(alog_e_ref[...]))
    a_cs = _split_dot(tri, a)
    a_last = a_cs[L - 1:L, :]
    dt_t = _softplus(gldt_ref[...].T[DT_LANE:DT_LANE + SSM_HEADS, :] + dtb_c_ref[...])
    a_cs_t = _split_dot_rhs(dt_t * (-jnp.exp(alog_c_ref[...])), tri_t)

    xdt = xs * dt
    xw = (xdt * jnp.exp(a_last - a_cs)).astype(BF16)
    exp_cs = jnp.exp(a_cs)
    lane = lax.broadcasted_iota(jnp.int32, (L, LANES), 1)
    rpg = SSM_HEADS // SSM_GROUPS
    gw = rpg * SSM_HEAD_DIM
    ys = []
    for g in range(SSM_GROUPS):
        bm_g = bm[:, g * SSM_STATE:(g + 1) * SSM_STATE]
        cm_g = cm[:, g * SSM_STATE:(g + 1) * SSM_STATE].astype(BF16)
        cb = _dot_nt(cm_g, bm_g.astype(BF16))
        prev = state_ref[:, g * gw:(g + 1) * gw]
        y_off = _dot(cm_g, prev.astype(BF16)) * exp_cs[:, g * gw:(g + 1) * gw]
        new_state = _dot(bm_g.T.astype(BF16), xw[:, g * gw:(g + 1) * gw])
        state_ref[:, g * gw:(g + 1) * gw] = prev * jnp.exp(a_last[:, g * gw:(g + 1) * gw]) + new_state
        for pair in range(rpg // 2):
            h0 = g * rpg + 2 * pair
            x_pair = xdt[:, h0 * SSM_HEAD_DIM:(h0 + 2) * SSM_HEAD_DIM].astype(BF16)
            halves = []
            for h in (h0, h0 + 1):
                diff = a_cs[:, h * SSM_HEAD_DIM:h * SSM_HEAD_DIM + 1] - a_cs_t[h:h + 1, :]
                decay = jnp.exp(jnp.where(causal, diff, NEG_INF))
                halves.append(_dot((cb * decay).astype(BF16), x_pair))
            ys.append(jnp.where(lane < SSM_HEAD_DIM, halves[0], halves[1])
                      + y_off[:, 2 * pair * SSM_HEAD_DIM:(2 * pair + 2) * SSM_HEAD_DIM])
    y = jnp.concatenate(ys, axis=-1) + xs * d_ref[...]
    y = y * _silu(z_ref[...])
    outs = []
    for g in range(SSM_GROUPS):
        yg = y[:, g * gw:(g + 1) * gw]
        outs.append(_rms(yg, nw_ref[:, g * gw:(g + 1) * gw]))
    o_ref[...] = jnp.concatenate(outs, axis=-1).astype(o_ref.dtype)


def _ssd(xbc, z, gldt, dte, conv_w, conv_b, dtb_e, alog_e, dtb_c, alog_c, d_e, norm_w, b, s):
    nc = s // SSM_CHUNK
    rows = lambda n: pl.BlockSpec((SSM_CHUNK, n), lambda r, c: (r * nc + c, 0))
    full = lambda arr: pl.BlockSpec(arr.shape, lambda r, c: (0,) * arr.ndim)
    consts = (conv_w, conv_b, dtb_e, alog_e, dtb_c, alog_c, d_e, norm_w)
    return pl.pallas_call(
        _ssd_body,
        grid=(b, nc),
        in_specs=[rows(SSM_CONV_DIM), rows(SSM_D_INNER), rows(LANES), rows(SSM_D_INNER)] + [full(a) for a in consts],
        out_specs=rows(SSM_D_INNER),
        out_shape=jax.ShapeDtypeStruct((b * s, SSM_D_INNER), BF16),
        scratch_shapes=[
            pltpu.VMEM((SSM_STATE, SSM_D_INNER), F32),
            pltpu.VMEM((CONV_PAD + SSM_CHUNK, SSM_CONV_DIM), F32),
        ],
        compiler_params=_params("parallel", "arbitrary"),
        name="ssd",
    )(xbc, z, gldt, dte, *consts)


def _even_mixer(h, gains, layer, w_in, pe, w1, w2, conv_w, conv_b, dt_bias, a_log, d_skip, norm_w, b, s):
    w_rows, w_cols = _prep_w_in(w_in)
    kvc, ks, kw, z, xbc, gldt, dte, q_t, vs_t, vw_t = _inproj(h, gains, w_rows, w_cols, _position_features(s), layer)
    kcb, vcb_t = _compress(kvc, *_prep_cmp(pe, w1, w2), b, s)
    o_attn = _attention(q_t, kcb, vcb_t, ks, vs_t, kw, vw_t, gldt, _overlap_t(s), b, s)
    rep = lambda v: jnp.repeat(v, SSM_HEAD_DIM)[None, :]
    o_ssm = _ssd(xbc, z, gldt, dte, conv_w, conv_b[None, :], rep(dt_bias), rep(a_log), dt_bias[:, None],
                 a_log[:, None], rep(d_skip), norm_w[None, :], b, s)
    return o_attn, o_ssm


def kernel(x, norm_gains, ffn_w_gate, ffn_w_up, ffn_w_down, ev_w_in, ev_w_out, nsa_cmp_pe, nsa_cmp_w1, nsa_cmp_w2,
           ssm_conv_w, ssm_conv_b, ssm_dt_bias, ssm_a_log, ssm_d, ssm_norm_w, od_w_in, od_ln_w, od_ln_b, od_w_s,
           od_b_s, od_w_out):
    b, s, d = x.shape
    depth = norm_gains.shape[0]
    h = x.reshape(b * s, d)
    wg = ffn_w_gate.astype(BF16)
    wu = ffn_w_up.astype(BF16)
    wd = ffn_w_down.astype(BF16)
    od_in = od_w_in.astype(BF16)
    od_out = od_w_out.astype(BF16)
    od_lnw = od_ln_w[:, None, :]
    od_lnb = od_ln_b[:, None, :]
    od_bst = jnp.swapaxes(od_b_s, 1, 2)
    for layer in range(depth):
        i = layer // 2
        if layer % 2 == 0:
            o_attn, o_ssm = _even_mixer(h, norm_gains, layer, ev_w_in[i], nsa_cmp_pe[i], nsa_cmp_w1[i],
                                        nsa_cmp_w2[i], ssm_conv_w[i], ssm_conv_b[i], ssm_dt_bias[i], ssm_a_log[i],
                                        ssm_d[i], ssm_norm_w[i], b, s)
            h = _ffn(h, norm_gains, wg, wu, wd, layer, (o_attn, o_ssm, ev_w_out[i].astype(BF16)))
        else:
            h = _gmlp(h, norm_gains, od_in, od_lnw, od_lnb, od_w_s, od_bst, od_out, layer, i)
            h = _ffn(h, norm_gains, wg, wu, wd, layer)
    return h.reshape(b, s, d)
```

```python
import functools

import numpy as np
import jax
import jax.numpy as jnp
from jax import lax
from jax.experimental import pallas as pl
from jax.experimental.pallas import tpu as pltpu

F32 = jnp.float32
BF16 = jnp.bfloat16

D_MODEL = 1024
RMS_EPS = 1e-6
LN_EPS = 1e-5
NEG_INF = -1e30

NSA_HEADS = 8
NSA_KV_HEADS = 2
NSA_GROUP = NSA_HEADS // NSA_KV_HEADS
HEAD_DIM = 64
CMP_BLOCK = 32
CMP_STRIDE = 16
CMP_HIDDEN = 256
SEL_BLOCK = 64
SEL_TOPK = 16
WINDOW = 512
FORCE_BONUS = 1e4

SSM_HEADS = 8
SSM_HEAD_DIM = 64
SSM_D_INNER = SSM_HEADS * SSM_HEAD_DIM
SSM_GROUPS = 2
SSM_STATE = 128
SSM_CONV = 4
SSM_CHUNK = 128
SSM_BC_DIM = SSM_GROUPS * SSM_STATE
SSM_CONV_DIM = SSM_D_INNER + 2 * SSM_BC_DIM

GMLP_WIDTH = 2 * D_MODEL
GMLP_GROUPS = 8
GMLP_GROUP_DIM = GMLP_WIDTH // GMLP_GROUPS
GMLP_CHUNK = 128

LANES = 128
VMEM_LIMIT = 56 * 1024 * 1024


def _params(*sem):
    return pltpu.CompilerParams(dimension_semantics=sem, vmem_limit_bytes=VMEM_LIMIT)


def _rms(x, g):
    return x * lax.rsqrt(jnp.mean(x * x, axis=-1, keepdims=True) + RMS_EPS) * g


def _silu(x):
    return x * jax.nn.sigmoid(x)


def _dot(a, b):
    return jnp.dot(a, b, preferred_element_type=F32)


def _dot_nt(a, b):
    return lax.dot_general(a, b, (((1,), (1,)), ((), ())), preferred_element_type=F32)


def _resident(block_shape, index_map):
    return pl.BlockSpec(block_shape, index_map, pipeline_mode=pl.Buffered(1))


FFN_TM = 512
FFN_CHUNK = 256


def _ffn_body(h_ref, g_ref, wg_ref, wu_ref, wd_ref, *rest):
    o_ref = rest[-1]
    x = h_ref[...]
    if len(rest) > 1:
        oa_ref, os_ref, wo_ref = rest[:3]
        na = oa_ref.shape[-1]
        x = x + _rms(_dot(oa_ref[...], wo_ref[:na, :]) + _dot(os_ref[...], wo_ref[na:, :]), g_ref[1:2, :])
    xn = _rms(x, g_ref[2:3, :]).astype(BF16)
    acc = jnp.zeros(x.shape, F32)
    for c in range(wg_ref.shape[1] // FFN_CHUNK):
        cols = slice(c * FFN_CHUNK, (c + 1) * FFN_CHUNK)
        a = _dot(xn, wg_ref[:, cols])
        b = _dot(xn, wu_ref[:, cols])
        acc = acc + _dot((_silu(a) * b).astype(BF16), wd_ref[cols, :])
    o_ref[...] = x + _rms(acc, g_ref[3:4, :])


def _ffn(h, gains, wg, wu, wd, layer, mixer_out=None):
    t = h.shape[0]
    hidden = wg.shape[2]
    rows = lambda n: pl.BlockSpec((FFN_TM, n), lambda i: (i, 0))
    in_specs = [
        rows(D_MODEL),
        pl.BlockSpec((None, 4, D_MODEL), lambda i: (layer, 0, 0)),
        _resident((None, D_MODEL, hidden), lambda i: (layer, 0, 0)),
        _resident((None, D_MODEL, hidden), lambda i: (layer, 0, 0)),
        _resident((None, hidden, D_MODEL), lambda i: (layer, 0, 0)),
    ]
    args = [h, gains, wg, wu, wd]
    if mixer_out is not None:
        oa, os_, w_out = mixer_out
        in_specs += [rows(oa.shape[1]), rows(os_.shape[1]), _resident(w_out.shape, lambda i: (0, 0))]
        args += [oa, os_, w_out]
    return pl.pallas_call(
        _ffn_body,
        grid=(t // FFN_TM,),
        in_specs=in_specs,
        out_specs=rows(D_MODEL),
        out_shape=jax.ShapeDtypeStruct((t, D_MODEL), F32),
        compiler_params=_params("parallel"),
        name="ffn",
    )(*args)


GMLP_TM = 512


def _gelu(x):
    return 0.5 * x * (1.0 + lax.erf(x * np.float32(1.0 / np.sqrt(2.0))))


GMLP_COLS = 1024


def _gmlp_body(h_ref, g_ref, win_ref, lnw_ref, lnb_ref, ws_ref, bs_ref, wout_ref, o_ref):
    x = h_ref[...]
    xn = _rms(x, g_ref[0:1, :]).astype(BF16)
    v = _gelu(_dot(xn, win_ref[:, GMLP_WIDTH:]))
    mu = jnp.mean(v, axis=-1, keepdims=True)
    vc = v - mu
    var = jnp.mean(vc * vc, axis=-1, keepdims=True)
    v = (vc * lax.rsqrt(var + LN_EPS) * lnw_ref[...] + lnb_ref[...]).astype(BF16)
    row = lax.broadcasted_iota(jnp.int32, (GMLP_CHUNK, GMLP_CHUNK), 0)
    col = lax.broadcasted_iota(jnp.int32, (GMLP_CHUNK, GMLP_CHUNK), 1)
    causal = col <= row
    y = jnp.zeros(x.shape, F32)
    for c0 in range(0, GMLP_WIDTH, GMLP_COLS):
        u = _gelu(_dot(xn, win_ref[:, c0:c0 + GMLP_COLS]))
        mixed = []
        for g in range(c0 // GMLP_GROUP_DIM, (c0 + GMLP_COLS) // GMLP_GROUP_DIM):
            ws = jnp.where(causal, ws_ref[g], 0.0).astype(BF16)
            bias = bs_ref[:, g:g + 1]
            cols = slice(g * GMLP_GROUP_DIM, (g + 1) * GMLP_GROUP_DIM)
            mixed.append(jnp.concatenate(
                [_dot(ws, v[c * GMLP_CHUNK:(c + 1) * GMLP_CHUNK, cols]) + bias for c in range(GMLP_TM // GMLP_CHUNK)],
                axis=0))
        gate = (u * jnp.concatenate(mixed, axis=1)).astype(BF16)
        y = y + _dot(gate, wout_ref[c0:c0 + GMLP_COLS, :])
    o_ref[...] = x + _rms(y, g_ref[1:2, :])


def _gmlp(h, gains, w_in, ln_w, ln_b, w_s, b_s_t, w_out, layer, i):
    t = h.shape[0]
    const2 = lambda r: (i, 0, 0)
    return pl.pallas_call(
        _gmlp_body,
        grid=(t // GMLP_TM,),
        in_specs=[
            pl.BlockSpec((GMLP_TM, D_MODEL), lambda r: (r, 0)),
            pl.BlockSpec((None, 4, D_MODEL), lambda r: (layer, 0, 0)),
            _resident((None, D_MODEL, 2 * GMLP_WIDTH), const2),
            pl.BlockSpec((None, 1, GMLP_WIDTH), const2),
            pl.BlockSpec((None, 1, GMLP_WIDTH), const2),
            pl.BlockSpec((None, GMLP_GROUPS, GMLP_CHUNK, GMLP_CHUNK), lambda r: (i, 0, 0, 0)),
            pl.BlockSpec((None, GMLP_CHUNK, GMLP_GROUPS), const2),
            _resident((None, GMLP_WIDTH, D_MODEL), const2),
        ],
        out_specs=pl.BlockSpec((GMLP_TM, D_MODEL), lambda r: (r, 0)),
        out_shape=jax.ShapeDtypeStruct((t, D_MODEL), F32),
        compiler_params=_params("parallel"),
        name="gmlp",
    )(h, gains, w_in, ln_w, ln_b, w_s, b_s_t, w_out)


PROJ_TM = 512
KV_PAD = NSA_KV_HEADS * LANES
BLOCK_LANE = HEAD_DIM
ALIBI_LANE = 96
V_ROWS = HEAD_DIM + 16
ONES_ROW = HEAD_DIM
ROW_SEGS = ((2 * LANES, F32, False),
            (KV_PAD, BF16, True),
            (KV_PAD, BF16, True),
            (SSM_D_INNER, F32, False),
            (SSM_CONV_DIM, F32, False),
            (LANES, F32, False))
COL_SEGS = ((NSA_HEADS * HEAD_DIM, False),
            (NSA_KV_HEADS * V_ROWS, True),
            (NSA_KV_HEADS * V_ROWS, True))
ROW_WIDTH = sum(seg[0] for seg in ROW_SEGS)
COL_WIDTH = sum(seg[0] for seg in COL_SEGS)
FEAT_WIDTH = sum(seg[0] for seg in ROW_SEGS if seg[2])


def _prep_w_in(w):
    nq = NSA_HEADS * HEAD_DIM
    nkv = NSA_KV_HEADS * HEAD_DIM

    def pad_groups(a, width):
        a = a.reshape(D_MODEL, NSA_KV_HEADS, HEAD_DIM)
        return jnp.pad(a, ((0, 0), (0, 0), (0, width - HEAD_DIM))).reshape(D_MODEL, NSA_KV_HEADS * width)

    o = nq
    kc, vc, ks, vs, kw, vw = [w[:, o + j * nkv:o + (j + 1) * nkv] for j in range(6)]
    o += 6 * nkv
    gl = w[:, o:o + 3 * NSA_HEADS]
    o += 3 * NSA_HEADS
    z = w[:, o:o + SSM_D_INNER]
    o += SSM_D_INNER
    xbc = w[:, o:o + SSM_CONV_DIM]
    o += SSM_CONV_DIM
    dt = w[:, o:o + SSM_HEADS]
    gldt = jnp.concatenate([gl, dt, jnp.zeros((D_MODEL, LANES - 3 * NSA_HEADS - SSM_HEADS), w.dtype)], axis=1)
    rows = jnp.concatenate([kc, vc, pad_groups(ks, LANES), pad_groups(kw, LANES), z, xbc, gldt], axis=1)
    cols = jnp.concatenate([w[:, :nq], pad_groups(vs, V_ROWS), pad_groups(vw, V_ROWS)], axis=1).T
    assert rows.shape[1] == ROW_WIDTH and cols.shape[0] == COL_WIDTH
    return rows.astype(BF16), cols.astype(BF16)


def _position_features(s):
    n_sel = s // SEL_BLOCK
    assert BLOCK_LANE + n_sel <= ALIBI_LANE and ALIBI_LANE + 2 <= LANES
    pos = np.arange(s)
    alibi = np.zeros((s, LANES), np.float32)
    alibi[:, ALIBI_LANE] = pos // SEL_BLOCK
    alibi[:, ALIBI_LANE + 1] = pos % SEL_BLOCK
    k_sel = alibi.copy()
    k_sel[pos, BLOCK_LANE + pos // SEL_BLOCK] = 1.0
    feats = [np.tile(a, (1, NSA_KV_HEADS)) for a in (k_sel, alibi)]
    return jnp.asarray(np.concatenate(feats, axis=1), BF16)


def _inproj_body(h_ref, g_ref, wr_ref, wc_ref, feat_ref, *out_refs):
    xn = _rms(h_ref[...], g_ref[0:1, :]).astype(BF16)
    row_refs = out_refs[:len(ROW_SEGS)]
    col_refs = out_refs[len(ROW_SEGS):]
    off = 0
    feat_off = 0
    for ref, (n, dtype, has_feat) in zip(row_refs, ROW_SEGS):
        val = _dot(xn, wr_ref[:, off:off + n]).astype(dtype)
        if has_feat:
            val = val + feat_ref[:, feat_off:feat_off + n]
            feat_off += n
        ref[...] = val
        off += n
    off = 0
    for ref, (n, has_ones) in zip(col_refs, COL_SEGS):
        val = _dot_nt(wc_ref[off:off + n, :], xn)
        if has_ones:
            row = lax.broadcasted_iota(jnp.int32, val.shape, 0)
            is_ones = row == ONES_ROW
            for g in range(1, NSA_KV_HEADS):
                is_ones = is_ones | (row == g * V_ROWS + ONES_ROW)
            val = jnp.where(is_ones, 1.0, val)
        ref[...] = val.astype(BF16)
        off += n


def _inproj(h, gains, w_rows, w_cols, feats, layer):
    t = h.shape[0]
    tiles_per_seq = feats.shape[0] // PROJ_TM
    return pl.pallas_call(
        _inproj_body,
        grid=(t // PROJ_TM,),
        in_specs=[
            pl.BlockSpec((PROJ_TM, D_MODEL), lambda r: (r, 0)),
            pl.BlockSpec((None, 4, D_MODEL), lambda r: (layer, 0, 0)),
            _resident((D_MODEL, ROW_WIDTH), lambda r: (0, 0)),
            _resident((COL_WIDTH, D_MODEL), lambda r: (0, 0)),
            pl.BlockSpec((PROJ_TM, FEAT_WIDTH), lambda r: (lax.rem(r, tiles_per_seq), 0)),
        ],
        out_specs=([pl.BlockSpec((PROJ_TM, seg[0]), lambda r: (r, 0)) for seg in ROW_SEGS]
                   + [pl.BlockSpec((seg[0], PROJ_TM), lambda r: (0, r)) for seg in COL_SEGS]),
        out_shape=([jax.ShapeDtypeStruct((t, seg[0]), seg[1]) for seg in ROW_SEGS]
                   + [jax.ShapeDtypeStruct((seg[0], t), BF16) for seg in COL_SEGS]),
        compiler_params=_params("parallel"),
        name="inproj",
    )(h, gains, w_rows, w_cols, feats)


N_CMP_PAD = 128
CMP_HALF = CMP_BLOCK // CMP_STRIDE


def _prep_cmp(pe, w1, w2):
    pe2 = jnp.concatenate([pe, pe], axis=-1)
    w1r = w1.reshape(2, CMP_HALF, CMP_STRIDE, HEAD_DIM, CMP_HIDDEN)
    w1d = jnp.concatenate([w1r, w1r], axis=3).reshape(2, CMP_HALF, CMP_STRIDE * LANES, CMP_HIDDEN)
    w2k = jnp.stack([jnp.pad(w2[0], ((0, 0), (g * LANES, KV_PAD - g * LANES - HEAD_DIM)))
                     for g in range(NSA_KV_HEADS)])
    return pe2, w1d.astype(BF16), w2k.astype(BF16), w2[1].T.astype(BF16)


def _compress_body(kc_ref, vc_ref, pe_ref, w1_ref, w2k_ref, w2v_ref, ok_ref, ov_ref):
    lane = lax.broadcasted_iota(jnp.int32, (N_CMP_PAD, LANES), 1)
    out_k = jnp.zeros((N_CMP_PAD, KV_PAD), F32)
    for kv in range(2):
        src = (kc_ref, vc_ref)[kv]
        pieces = [src[pl.ds(l, N_CMP_PAD, stride=CMP_STRIDE), :] for l in range(CMP_STRIDE)]
        for g in range(NSA_KV_HEADS):
            in_group = (lane >= g * HEAD_DIM) & (lane < (g + 1) * HEAD_DIM)
            pre = None
            for half in range(CMP_HALF):
                xs = [jnp.where(in_group, pieces[l] + pe_ref[kv, half * CMP_STRIDE + l:half * CMP_STRIDE + l + 1, :], 0.0)
                      for l in range(CMP_STRIDE)]
                xcat = jnp.concatenate(xs, axis=-1).astype(BF16)
                part = _dot(xcat, w1_ref[kv, half])
                if half == 1:
                    part = pltpu.roll(part, N_CMP_PAD - 1, 0)
                pre = part if pre is None else pre + part
            hid = _silu(pre).astype(BF16)
            if kv == 0:
                out_k = out_k + _dot(hid, w2k_ref[g])
            else:
                ov_ref[g * HEAD_DIM:(g + 1) * HEAD_DIM, :] = _dot_nt(w2v_ref[...], hid).astype(BF16)
    ok_ref[...] = out_k.astype(BF16)


def _compress(kvc, pe2, w1d, w2k, w2v, b, s):
    return pl.pallas_call(
        _compress_body,
        grid=(b,),
        in_specs=[
            pl.BlockSpec((s, LANES), lambda r: (r, 0)),
            pl.BlockSpec((s, LANES), lambda r: (r, 1)),
            pl.BlockSpec(pe2.shape, lambda r: (0, 0, 0)),
            pl.BlockSpec(w1d.shape, lambda r: (0, 0, 0, 0)),
            pl.BlockSpec(w2k.shape, lambda r: (0, 0, 0)),
            pl.BlockSpec(w2v.shape, lambda r: (0, 0)),
        ],
        out_specs=[pl.BlockSpec((None, N_CMP_PAD, KV_PAD), lambda r: (r, 0, 0)),
                   pl.BlockSpec((None, NSA_KV_HEADS * HEAD_DIM, N_CMP_PAD), lambda r: (r, 0, 0))],
        out_shape=[jax.ShapeDtypeStruct((b, N_CMP_PAD, KV_PAD), BF16),
                   jax.ShapeDtypeStruct((b, NSA_KV_HEADS * HEAD_DIM, N_CMP_PAD), BF16)],
        compiler_params=_params("parallel"),
        name="compress",
    )(kvc, kvc, pe2, w1d, w2k, w2v)


ATT_T = 256
SEL_SHIFT = SEL_BLOCK.bit_length() - 1
MASK_DIST = 2.0 ** 100
MASK_BIAS = -(2.0 ** 100)
WIN_KEYS = WINDOW + ATT_T
DIAG_KEYS = 512
TILES_PER_DIAG = DIAG_KEYS // ATT_T
HEAD_PACK = 4


def _overlap_t(s):
    n_cmp = (s - CMP_BLOCK) // CMP_STRIDE + 1
    n_sel = s // SEL_BLOCK
    assert n_cmp <= N_CMP_PAD
    cs = np.arange(n_cmp)[None, :] * CMP_STRIDE
    ss = np.arange(n_sel)[:, None] * SEL_BLOCK
    ov = np.clip(np.minimum(cs + CMP_BLOCK, ss + SEL_BLOCK) - np.maximum(cs, ss), 0, None) / CMP_BLOCK
    return jnp.asarray(np.pad(ov, ((0, 0), (0, N_CMP_PAD - n_cmp))), BF16)


def _softmax_pv_t(sc, v_t):
    p = jnp.exp(sc - jnp.max(sc, axis=0, keepdims=True))
    o = _dot(v_t, p.astype(BF16))
    return o[0:HEAD_DIM, :] * (1.0 / o[ONES_ROW:ONES_ROW + 1, :])


def _attn_body(q_ref, kc_ref, vc_ref, ks_ref, vs_ref, kw_ref, vw_ref, gl_ref, ovt_ref, o_ref, out_ref,
               *, n_sel, n_variants):
    i = pl.program_id(1)
    t0 = i * ATT_T
    gates = jax.nn.sigmoid(gl_ref[...].T)

    def gate(hh, branch):
        return gates[3 * hh + branch:3 * hh + branch + 1, :]

    def rel(n):
        return lax.broadcasted_iota(jnp.int32, (n, ATT_T), 1) - lax.broadcasted_iota(jnp.int32, (n, ATT_T), 0)

    w0 = pl.multiple_of(jnp.maximum(t0 - WINDOW, 0), ATT_T)
    dist_w = (t0 - w0) + rel(WIN_KEYS)
    bias_w = jnp.where((dist_w >= 0) & (dist_w < WINDOW), 0.0, MASK_BIAS)

    blk_c = lax.broadcasted_iota(jnp.int32, (N_CMP_PAD, ATT_T), 0)
    d_cmp = t0 + lax.broadcasted_iota(jnp.int32, (N_CMP_PAD, ATT_T), 1) - (blk_c * CMP_STRIDE + (CMP_BLOCK - 1))
    valid_c = d_cmp >= 0
    dm_c = jnp.where(valid_c, d_cmp.astype(F32), MASK_DIST)

    blk = lax.broadcasted_iota(jnp.int32, (n_sel, ATT_T), 0)
    tq = t0 + lax.broadcasted_iota(jnp.int32, (n_sel, ATT_T), 1)
    cur = tq >> SEL_SHIFT
    forced = (blk == 0) | (blk == cur) | (blk == cur - 1)
    eligible = blk * SEL_BLOCK <= tq
    feat_row = lax.broadcasted_iota(jnp.int32, (LANES - BLOCK_LANE - n_sel, ATT_T), 0)

    def lanes(parts):
        return jnp.concatenate(parts, axis=1)

    def tiled(a):
        return lanes([a] * HEAD_PACK)

    bias_w = tiled(bias_w)
    dm_c = tiled(dm_c)
    valid_c = tiled(valid_c)
    pad = jnp.zeros((LANES - HEAD_DIM, ATT_T), BF16)

    def gate_row(heads, branch):
        return lanes([gate(hh, branch) for hh in heads])

    packs = []
    for g in range(NSA_KV_HEADS):
        group = slice(g * LANES, (g + 1) * LANES)
        group_packs = [[g * NSA_GROUP + k + h for h in range(HEAD_PACK)] for k in range(0, NSA_GROUP, HEAD_PACK)]
        slopes = [[2.0 ** -(hh + 1) for hh in heads] for heads in group_packs]
        qs = [[q_ref[hh * HEAD_DIM:(hh + 1) * HEAD_DIM, :] * (HEAD_DIM ** -0.5) for hh in heads]
              for heads in group_packs]

        kc = kc_ref[:, group]
        vc_t = vc_ref[g * HEAD_DIM:(g + 1) * HEAD_DIM, :]
        psum = jnp.zeros((N_CMP_PAD, ATT_T), F32)
        for pk, heads in enumerate(group_packs):
            slope_row = lanes([jnp.full((1, ATT_T), sl, F32) for sl in slopes[pk]])
            sc = _dot(kc, lanes([jnp.concatenate([q, pad], axis=0) for q in qs[pk]])) - slope_row * dm_c
            e = jnp.exp(sc - jnp.max(sc, axis=0, keepdims=True))
            p = e * (1.0 / jnp.sum(e, axis=0, keepdims=True))
            p = jnp.where(valid_c, p, 0.0)
            for h in range(HEAD_PACK):
                psum = psum + p[:, h * ATT_T:(h + 1) * ATT_T]
            out_ref[heads[0] // HEAD_PACK] = gate_row(heads, 0) * _dot(vc_t, p.astype(BF16))

        p_hi = psum.astype(BF16)
        p_lo = (psum - p_hi.astype(F32)).astype(BF16)
        imp = _dot(ovt_ref[...], p_hi) + _dot(ovt_ref[...], p_lo)
        imp = jnp.where(eligible, imp + jnp.where(forced, FORCE_BONUS, 0.0), -1.0)
        rank = jnp.zeros((n_sel, ATT_T), F32)
        for j in range(n_sel):
            cj = imp[j:j + 1, :]
            tie = jnp.where(blk > j, 1.0, 0.0)
            rank = rank + jnp.where(cj > imp, 1.0, jnp.where(cj == imp, tie, 0.0))
        drop = jnp.where(rank < min(SEL_TOPK, n_sel), 0.0, MASK_BIAS).astype(BF16)

        for pk, heads in enumerate(group_packs):
            cols = []
            for q, sl in zip(qs[pk], slopes[pk]):
                alibi = jnp.where(feat_row == ALIBI_LANE - BLOCK_LANE - n_sel, sl * SEL_BLOCK,
                                  jnp.where(feat_row == ALIBI_LANE + 1 - BLOCK_LANE - n_sel, sl, 0.0))
                cols.append(jnp.concatenate([q, drop, alibi.astype(BF16)], axis=0))
            packs.append((g, heads, lanes(cols)))

    def attend(branch, keys_of, values_of, masked):
        def finish(pack, sc):
            g, heads, _ = pack
            out_ref[heads[0] // HEAD_PACK] += gate_row(heads, branch) * _softmax_pv_t(masked(sc), values_of(g))

        pending = None
        for pack in packs:
            sc = _dot(keys_of(pack[0]), pack[2])
            if pending is not None:
                finish(*pending)
            pending = (pack, sc)
        finish(*pending)

    def selected(n_keys):
        causal = tiled(jnp.where(rel(DIAG_KEYS) + (t0 - (n_keys - DIAG_KEYS)) >= 0, 0.0, MASK_BIAS))

        def masked(sc):
            tail = sc[n_keys - DIAG_KEYS:, :] + causal
            return tail if n_keys == DIAG_KEYS else jnp.concatenate([sc[:n_keys - DIAG_KEYS, :], tail], axis=0)

        attend(1, lambda g: ks_ref[0:n_keys, g * LANES:(g + 1) * LANES],
               lambda g: vs_ref[g * V_ROWS:(g + 1) * V_ROWS, 0:n_keys], masked)

    for v in range(n_variants):
        pl.when(lax.div(i, TILES_PER_DIAG) == v)(functools.partial(selected, (v + 1) * DIAG_KEYS))

    attend(2, lambda g: kw_ref[pl.ds(w0, WIN_KEYS), g * LANES:(g + 1) * LANES],
           lambda g: vw_ref[g * V_ROWS:(g + 1) * V_ROWS, pl.ds(w0, WIN_KEYS)], lambda sc: sc + bias_w)

    o_t = [out_ref[hh // HEAD_PACK][:, (hh % HEAD_PACK) * ATT_T:(hh % HEAD_PACK + 1) * ATT_T]
           for hh in range(NSA_HEADS)]
    o_ref[...] = jnp.concatenate(o_t, axis=0).T.astype(o_ref.dtype)


def _attention(q_t, kcb, vcb_t, ks, vs_t, kw, vw_t, gldt, ovt, b, s):
    nq = s // ATT_T
    n_sel = s // SEL_BLOCK
    assert DIAG_KEYS % ATT_T == 0 and s % DIAG_KEYS == 0 and WINDOW % ATT_T == 0
    keys = pl.BlockSpec((s, KV_PAD), lambda r, i: (r, 0))
    vals_t = pl.BlockSpec((NSA_KV_HEADS * V_ROWS, s), lambda r, i: (0, r))
    return pl.pallas_call(
        functools.partial(_attn_body, n_sel=n_sel, n_variants=s // DIAG_KEYS),
        grid=(b, nq),
        in_specs=[
            pl.BlockSpec((NSA_HEADS * HEAD_DIM, ATT_T), lambda r, i: (0, r * nq + i)),
            pl.BlockSpec((None, N_CMP_PAD, KV_PAD), lambda r, i: (r, 0, 0)),
            pl.BlockSpec((None, NSA_KV_HEADS * HEAD_DIM, N_CMP_PAD), lambda r, i: (r, 0, 0)),
            keys, vals_t, keys, vals_t,
            pl.BlockSpec((ATT_T, LANES), lambda r, i: (r * nq + i, 0)),
            pl.BlockSpec(ovt.shape, lambda r, i: (0, 0)),
        ],
        out_specs=pl.BlockSpec((ATT_T, NSA_HEADS * HEAD_DIM), lambda r, i: (r * nq + i, 0)),
        out_shape=jax.ShapeDtypeStruct((b * s, NSA_HEADS * HEAD_DIM), BF16),
        scratch_shapes=[pltpu.VMEM((NSA_HEADS // HEAD_PACK, HEAD_DIM, HEAD_PACK * ATT_T), F32)],
        compiler_params=_params("parallel", "parallel"),
        name="nsa_attention",
    )(q_t, kcb, vcb_t, ks, vs_t, kw, vw_t, gldt, ovt)


CONV_PAD = 8
DT_LANE = 3 * NSA_HEADS


def _split_dot(a, b):
    b0 = b.astype(BF16)
    r1 = b - b0.astype(F32)
    b1 = r1.astype(BF16)
    b2 = (r1 - b1.astype(F32)).astype(BF16)
    return _dot(a, b0) + _dot(a, b1) + _dot(a, b2)


def _split_dot_rhs(a, b):
    a0 = a.astype(BF16)
    r1 = a - a0.astype(F32)
    a1 = r1.astype(BF16)
    a2 = (r1 - a1.astype(F32)).astype(BF16)
    return _dot(a0, b) + _dot(a1, b) + _dot(a2, b)


def _softplus(x):
    return jnp.maximum(x, 0.0) + jnp.log1p(jnp.exp(-jnp.abs(x)))


def _ssd_consts(dt_bias, a_log):
    lanes = np.arange(DT_LANE, DT_LANE + SSM_HEADS)
    dtb = jnp.zeros((1, LANES), F32).at[0, lanes].set(dt_bias)
    alog = jnp.zeros((1, LANES), F32).at[0, lanes].set(a_log)
    expand = np.zeros((LANES, SSM_D_INNER), np.float32)
    for h in range(SSM_HEADS):
        expand[DT_LANE + h, h * SSM_HEAD_DIM:(h + 1) * SSM_HEAD_DIM] = 1.0
    return dtb, alog, jnp.asarray(expand, BF16)


SSD_CHUNKS = 4
SSD_ROWS = SSD_CHUNKS * SSM_CHUNK


def _ssd_body(xbc_ref, z_ref, gldt_ref, cw_ref, cb_ref, dtb_ref, alog_ref, expand_ref, d_ref, nw_ref, o_ref,
              state_ref, xe_ref):
    step = pl.program_id(1)
    L = SSM_CHUNK

    @pl.when(step == 0)
    def _():
        state_ref[...] = jnp.zeros(state_ref.shape, F32)
        xe_ref[0:CONV_PAD, :] = jnp.zeros((CONV_PAD, SSM_CONV_DIM), F32)

    x_raw = xbc_ref[...]
    xe_ref[CONV_PAD:CONV_PAD + SSD_ROWS, :] = x_raw
    acc = jnp.zeros((SSD_ROWS, SSM_CONV_DIM), F32)
    for k in range(SSM_CONV):
        lo = CONV_PAD - (SSM_CONV - 1) + k
        acc = acc + xe_ref[lo:lo + SSD_ROWS, :] * cw_ref[k:k + 1, :]
    xe_ref[0:CONV_PAD, :] = x_raw[SSD_ROWS - CONV_PAD:, :]
    xa = _silu(acc + cb_ref[...])

    row = lax.broadcasted_iota(jnp.int32, (L, L), 0)
    col = lax.broadcasted_iota(jnp.int32, (L, L), 1)
    causal = col <= row
    tri = jnp.where(causal, 1.0, 0.0).astype(BF16)
    lane = lax.broadcasted_iota(jnp.int32, (L, LANES), 1)
    rpg = SSM_HEADS // SSM_GROUPS
    gw = rpg * SSM_HEAD_DIM

    lane128 = lax.broadcasted_iota(jnp.int32, (1, LANES), 1)
    on_dt = (lane128 >= DT_LANE) & (lane128 < DT_LANE + SSM_HEADS)
    dt_all = _softplus(gldt_ref[...] + dtb_ref[...])
    a_all = dt_all * jnp.where(on_dt, -jnp.exp(alog_ref[...]), 0.0)
    z_act = _silu(z_ref[...])

    for j in range(SSD_CHUNKS):
        rows = slice(j * L, (j + 1) * L)
        xs = xa[rows, :SSM_D_INNER]
        bm = xa[rows, SSM_D_INNER:SSM_D_INNER + SSM_BC_DIM]
        cm = xa[rows, SSM_D_INNER + SSM_BC_DIM:]
        a_cs_c = _split_dot(tri, a_all[rows, :])
        a_cs_t = a_cs_c.T[DT_LANE:DT_LANE + SSM_HEADS, :]
        dt = _split_dot_rhs(dt_all[rows, :], expand_ref[...])
        a_cs = _split_dot_rhs(a_cs_c, expand_ref[...])
        a_last = a_cs[L - 1:L, :]

        xdt = xs * dt
        xw = (xdt * jnp.exp(a_last - a_cs)).astype(BF16)
        exp_cs = jnp.exp(a_cs)
        ys = []
        for g in range(SSM_GROUPS):
            bm_g = bm[:, g * SSM_STATE:(g + 1) * SSM_STATE]
            cm_g = cm[:, g * SSM_STATE:(g + 1) * SSM_STATE].astype(BF16)
            cb = _dot_nt(cm_g, bm_g.astype(BF16))
            prev = state_ref[:, g * gw:(g + 1) * gw]
            y_off = _dot(cm_g, prev.astype(BF16)) * exp_cs[:, g * gw:(g + 1) * gw]
            new_state = _dot(bm_g.T.astype(BF16), xw[:, g * gw:(g + 1) * gw])
            state_ref[:, g * gw:(g + 1) * gw] = prev * jnp.exp(a_last[:, g * gw:(g + 1) * gw]) + new_state
            for pair in range(rpg // 2):
                h0 = g * rpg + 2 * pair
                x_pair = xdt[:, h0 * SSM_HEAD_DIM:(h0 + 2) * SSM_HEAD_DIM].astype(BF16)
                halves = []
                for h in (h0, h0 + 1):
                    diff = a_cs[:, h * SSM_HEAD_DIM:h * SSM_HEAD_DIM + 1] - a_cs_t[h:h + 1, :]
                    decay = jnp.exp(jnp.where(causal, diff, NEG_INF))
                    halves.append(_dot((cb * decay).astype(BF16), x_pair))
                ys.append(jnp.where(lane < SSM_HEAD_DIM, halves[0], halves[1])
                          + y_off[:, 2 * pair * SSM_HEAD_DIM:(2 * pair + 2) * SSM_HEAD_DIM])
        y = jnp.concatenate(ys, axis=-1) + xs * d_ref[...]
        y = y * z_act[rows, :]
        outs = []
        for g in range(SSM_GROUPS):
            yg = y[:, g * gw:(g + 1) * gw]
            outs.append(_rms(yg, nw_ref[:, g * gw:(g + 1) * gw]))
        o_ref[rows, :] = jnp.concatenate(outs, axis=-1).astype(o_ref.dtype)


def _ssd(xbc, z, gldt, conv_w, conv_b, dt_bias, a_log, d_skip, norm_w, b, s):
    nc = s // SSD_ROWS
    rows = lambda n: pl.BlockSpec((SSD_ROWS, n), lambda r, c: (r * nc + c, 0))
    full = lambda arr: pl.BlockSpec(arr.shape, lambda r, c: (0,) * arr.ndim)
    consts = (conv_w, conv_b[None, :], *_ssd_consts(dt_bias, a_log), jnp.repeat(d_skip, SSM_HEAD_DIM)[None, :],
              norm_w[None, :])
    return pl.pallas_call(
        _ssd_body,
        grid=(b, nc),
        in_specs=[rows(SSM_CONV_DIM), rows(SSM_D_INNER), rows(LANES)] + [full(a) for a in consts],
        out_specs=rows(SSM_D_INNER),
        out_shape=jax.ShapeDtypeStruct((b * s, SSM_D_INNER), BF16),
        scratch_shapes=[
            pltpu.VMEM((SSM_STATE, SSM_D_INNER), F32),
            pltpu.VMEM((CONV_PAD + SSD_ROWS, SSM_CONV_DIM), F32),
        ],
        compiler_params=_params("parallel", "arbitrary"),
        name="ssd",
    )(xbc, z, gldt, *consts)


def _even_mixer(h, gains, layer, w_in, pe, w1, w2, conv_w, conv_b, dt_bias, a_log, d_skip, norm_w, b, s):
    w_rows, w_cols = _prep_w_in(w_in)
    kvc, ks, kw, z, xbc, gldt, q_t, vs_t, vw_t = _inproj(h, gains, w_rows, w_cols, _position_features(s), layer)
    kcb, vcb_t = _compress(kvc, *_prep_cmp(pe, w1, w2), b, s)
    o_attn = _attention(q_t, kcb, vcb_t, ks, vs_t, kw, vw_t, gldt, _overlap_t(s), b, s)
    o_ssm = _ssd(xbc, z, gldt, conv_w, conv_b, dt_bias, a_log, d_skip, norm_w, b, s)
    return o_attn, o_ssm


def kernel(x, norm_gains, ffn_w_gate, ffn_w_up, ffn_w_down, ev_w_in, ev_w_out, nsa_cmp_pe, nsa_cmp_w1, nsa_cmp_w2,
           ssm_conv_w, ssm_conv_b, ssm_dt_bias, ssm_a_log, ssm_d, ssm_norm_w, od_w_in, od_ln_w, od_ln_b, od_w_s,
           od_b_s, od_w_out):
    b, s, d = x.shape
    depth = norm_gains.shape[0]
    h = x.reshape(b * s, d)
    wg = ffn_w_gate.astype(BF16)
    wu = ffn_w_up.astype(BF16)
    wd = ffn_w_down.astype(BF16)
    od_in = od_w_in.astype(BF16)
    od_out = od_w_out.astype(BF16)
    od_lnw = od_ln_w[:, None, :]
    od_lnb = od_ln_b[:, None, :]
    od_bst = jnp.swapaxes(od_b_s, 1, 2)
    for layer in range(depth):
        i = layer // 2
        if layer % 2 == 0:
            o_attn, o_ssm = _even_mixer(h, norm_gains, layer, ev_w_in[i], nsa_cmp_pe[i], nsa_cmp_w1[i],
                                        nsa_cmp_w2[i], ssm_conv_w[i], ssm_conv_b[i], ssm_dt_bias[i], ssm_a_log[i],
                                        ssm_d[i], ssm_norm_w[i], b, s)
            h = _ffn(h, norm_gains, wg, wu, wd, layer, (o_attn, o_ssm, ev_w_out[i].astype(BF16)))
        else:
            h = _gmlp(h, norm_gains, od_in, od_lnw, od_lnb, od_w_s, od_bst, od_out, layer, i)
            h = _ffn(h, norm_gains, wg, wu, wd, layer)
    return h.reshape(b, s, d)
```

```python
import functools

import numpy as np
import jax
import jax.numpy as jnp
from jax import lax
from jax.experimental import pallas as pl
from jax.experimental.pallas import tpu as pltpu

F32 = jnp.float32
BF16 = jnp.bfloat16

D_MODEL = 1024
RMS_EPS = 1e-6
LN_EPS = 1e-5
NEG_INF = -1e30

NSA_HEADS = 8
NSA_KV_HEADS = 2
NSA_GROUP = NSA_HEADS // NSA_KV_HEADS
HEAD_DIM = 64
CMP_BLOCK = 32
CMP_STRIDE = 16
CMP_HIDDEN = 256
SEL_BLOCK = 64
SEL_TOPK = 16
WINDOW = 512
FORCE_BONUS = 1e4

SSM_HEADS = 8
SSM_HEAD_DIM = 64
SSM_D_INNER = SSM_HEADS * SSM_HEAD_DIM
SSM_GROUPS = 2
SSM_STATE = 128
SSM_CONV = 4
SSM_CHUNK = 128
SSM_BC_DIM = SSM_GROUPS * SSM_STATE
SSM_CONV_DIM = SSM_D_INNER + 2 * SSM_BC_DIM

GMLP_WIDTH = 2 * D_MODEL
GMLP_GROUPS = 8
GMLP_GROUP_DIM = GMLP_WIDTH // GMLP_GROUPS
GMLP_CHUNK = 128

LANES = 128
VMEM_LIMIT = 56 * 1024 * 1024


def _params(*sem):
    return pltpu.CompilerParams(dimension_semantics=sem, vmem_limit_bytes=VMEM_LIMIT)


def _rms(x, g):
    return x * lax.rsqrt(jnp.mean(x * x, axis=-1, keepdims=True) + RMS_EPS) * g


def _silu(x):
    return x * jax.nn.sigmoid(x)


def _dot(a, b):
    return jnp.dot(a, b, preferred_element_type=F32)


def _dot_nt(a, b):
    return lax.dot_general(a, b, (((1,), (1,)), ((), ())), preferred_element_type=F32)


def _resident(block_shape, index_map):
    return pl.BlockSpec(block_shape, index_map, pipeline_mode=pl.Buffered(1))


FFN_TM = 512
FFN_CHUNK = 256


def _ffn_body(h_ref, g_ref, wg_ref, wu_ref, wd_ref, *rest):
    o_ref = rest[-1]
    x = h_ref[...]
    if len(rest) > 1:
        oa_ref, os_ref, wo_ref = rest[:3]
        na = oa_ref.shape[-1]
        x = x + _rms(_dot(oa_ref[...], wo_ref[:na, :]) + _dot(os_ref[...], wo_ref[na:, :]), g_ref[1:2, :])
    xn = _rms(x, g_ref[2:3, :]).astype(BF16)
    acc = jnp.zeros(x.shape, F32)
    for c in range(wg_ref.shape[1] // FFN_CHUNK):
        cols = slice(c * FFN_CHUNK, (c + 1) * FFN_CHUNK)
        a = _dot(xn, wg_ref[:, cols])
        b = _dot(xn, wu_ref[:, cols])
        acc = acc + _dot((_silu(a) * b).astype(BF16), wd_ref[cols, :])
    o_ref[...] = x + _rms(acc, g_ref[3:4, :])


def _ffn(h, gains, wg, wu, wd, layer, mixer_out=None):
    t = h.shape[0]
    hidden = wg.shape[2]
    assert hidden % FFN_CHUNK == 0 and t % FFN_TM == 0
    rows = lambda n: pl.BlockSpec((FFN_TM, n), lambda i: (i, 0))
    in_specs = [
        rows(D_MODEL),
        pl.BlockSpec((None, 4, D_MODEL), lambda i: (layer, 0, 0)),
        _resident((None, D_MODEL, hidden), lambda i: (layer, 0, 0)),
        _resident((None, D_MODEL, hidden), lambda i: (layer, 0, 0)),
        _resident((None, hidden, D_MODEL), lambda i: (layer, 0, 0)),
    ]
    args = [h, gains, wg, wu, wd]
    if mixer_out is not None:
        oa, os_, w_out = mixer_out
        in_specs += [rows(oa.shape[1]), rows(os_.shape[1]), _resident(w_out.shape, lambda i: (0, 0))]
        args += [oa, os_, w_out]
    return pl.pallas_call(
        _ffn_body,
        grid=(t // FFN_TM,),
        in_specs=in_specs,
        out_specs=rows(D_MODEL),
        out_shape=jax.ShapeDtypeStruct((t, D_MODEL), F32),
        compiler_params=_params("parallel"),
        name="ffn",
    )(*args)


GMLP_TM = 512


def _gelu(x):
    return 0.5 * x * (1.0 + lax.erf(x * np.float32(1.0 / np.sqrt(2.0))))


GMLP_COLS = 1024


def _gmlp_body(h_ref, g_ref, win_ref, lnw_ref, lnb_ref, ws_ref, bs_ref, wout_ref, o_ref):
    x = h_ref[...]
    xn = _rms(x, g_ref[0:1, :]).astype(BF16)
    v = _gelu(_dot(xn, win_ref[:, GMLP_WIDTH:]))
    mu = jnp.mean(v, axis=-1, keepdims=True)
    vc = v - mu
    var = jnp.mean(vc * vc, axis=-1, keepdims=True)
    v = (vc * lax.rsqrt(var + LN_EPS) * lnw_ref[...] + lnb_ref[...]).astype(BF16)
    row = lax.broadcasted_iota(jnp.int32, (GMLP_CHUNK, GMLP_CHUNK), 0)
    col = lax.broadcasted_iota(jnp.int32, (GMLP_CHUNK, GMLP_CHUNK), 1)
    causal = col <= row
    y = jnp.zeros(x.shape, F32)
    for c0 in range(0, GMLP_WIDTH, GMLP_COLS):
        u = _gelu(_dot(xn, win_ref[:, c0:c0 + GMLP_COLS]))
        mixed = []
        for g in range(c0 // GMLP_GROUP_DIM, (c0 + GMLP_COLS) // GMLP_GROUP_DIM):
            ws = jnp.where(causal, ws_ref[g], 0.0).astype(BF16)
            bias = bs_ref[:, g:g + 1]
            cols = slice(g * GMLP_GROUP_DIM, (g + 1) * GMLP_GROUP_DIM)
            mixed.append(jnp.concatenate(
                [_dot(ws, v[c * GMLP_CHUNK:(c + 1) * GMLP_CHUNK, cols]) + bias for c in range(GMLP_TM // GMLP_CHUNK)],
                axis=0))
        gate = (u * jnp.concatenate(mixed, axis=1)).astype(BF16)
        y = y + _dot(gate, wout_ref[c0:c0 + GMLP_COLS, :])
    o_ref[...] = x + _rms(y, g_ref[1:2, :])


def _gmlp(h, gains, w_in, ln_w, ln_b, w_s, b_s_t, w_out, layer, i):
    t = h.shape[0]
    const2 = lambda r: (i, 0, 0)
    return pl.pallas_call(
        _gmlp_body,
        grid=(t // GMLP_TM,),
        in_specs=[
            pl.BlockSpec((GMLP_TM, D_MODEL), lambda r: (r, 0)),
            pl.BlockSpec((None, 4, D_MODEL), lambda r: (layer, 0, 0)),
            _resident((None, D_MODEL, 2 * GMLP_WIDTH), const2),
            pl.BlockSpec((None, 1, GMLP_WIDTH), const2),
            pl.BlockSpec((None, 1, GMLP_WIDTH), const2),
            pl.BlockSpec((None, GMLP_GROUPS, GMLP_CHUNK, GMLP_CHUNK), lambda r: (i, 0, 0, 0)),
            pl.BlockSpec((None, GMLP_CHUNK, GMLP_GROUPS), const2),
            _resident((None, GMLP_WIDTH, D_MODEL), const2),
        ],
        out_specs=pl.BlockSpec((GMLP_TM, D_MODEL), lambda r: (r, 0)),
        out_shape=jax.ShapeDtypeStruct((t, D_MODEL), F32),
        compiler_params=_params("parallel"),
        name="gmlp",
    )(h, gains, w_in, ln_w, ln_b, w_s, b_s_t, w_out)


PROJ_TM = 512
KV_PAD = NSA_KV_HEADS * LANES
BLOCK_LANE = HEAD_DIM
ALIBI_LANE = 96
V_ROWS = HEAD_DIM + 16
ONES_ROW = HEAD_DIM
ROW_SEGS = ((2 * LANES, F32, False),
            (KV_PAD, BF16, True),
            (KV_PAD, BF16, True),
            (SSM_D_INNER, F32, False),
            (SSM_CONV_DIM, F32, False),
            (LANES, F32, False))
COL_SEGS = ((NSA_HEADS * HEAD_DIM, False),
            (NSA_KV_HEADS * V_ROWS, True),
            (NSA_KV_HEADS * V_ROWS, True))
ROW_WIDTH = sum(seg[0] for seg in ROW_SEGS)
COL_WIDTH = sum(seg[0] for seg in COL_SEGS)
FEAT_WIDTH = sum(seg[0] for seg in ROW_SEGS if seg[2])


def _prep_w_in(w):
    nq = NSA_HEADS * HEAD_DIM
    nkv = NSA_KV_HEADS * HEAD_DIM

    def pad_groups(a, width):
        a = a.reshape(D_MODEL, NSA_KV_HEADS, HEAD_DIM)
        return jnp.pad(a, ((0, 0), (0, 0), (0, width - HEAD_DIM))).reshape(D_MODEL, NSA_KV_HEADS * width)

    o = nq
    kc, vc, ks, vs, kw, vw = [w[:, o + j * nkv:o + (j + 1) * nkv] for j in range(6)]
    o += 6 * nkv
    gl = w[:, o:o + 3 * NSA_HEADS]
    o += 3 * NSA_HEADS
    z = w[:, o:o + SSM_D_INNER]
    o += SSM_D_INNER
    xbc = w[:, o:o + SSM_CONV_DIM]
    o += SSM_CONV_DIM
    dt = w[:, o:o + SSM_HEADS]
    gldt = jnp.concatenate([gl, dt, jnp.zeros((D_MODEL, LANES - 3 * NSA_HEADS - SSM_HEADS), w.dtype)], axis=1)
    rows = jnp.concatenate([kc, vc, pad_groups(ks, LANES), pad_groups(kw, LANES), z, xbc, gldt], axis=1)
    cols = jnp.concatenate([w[:, :nq], pad_groups(vs, V_ROWS), pad_groups(vw, V_ROWS)], axis=1).T
    assert rows.shape[1] == ROW_WIDTH and cols.shape[0] == COL_WIDTH
    return rows.astype(BF16), cols.astype(BF16)


def _position_features(s):
    n_sel = s // SEL_BLOCK
    assert BLOCK_LANE + n_sel <= ALIBI_LANE and ALIBI_LANE + 2 <= LANES
    pos = np.arange(s)
    alibi = np.zeros((s, LANES), np.float32)
    alibi[:, ALIBI_LANE] = pos // SEL_BLOCK
    alibi[:, ALIBI_LANE + 1] = pos % SEL_BLOCK
    k_sel = alibi.copy()
    k_sel[pos, BLOCK_LANE + pos // SEL_BLOCK] = 1.0
    feats = [np.tile(a, (1, NSA_KV_HEADS)) for a in (k_sel, alibi)]
    return jnp.asarray(np.concatenate(feats, axis=1), BF16)


def _inproj_body(h_ref, g_ref, wr_ref, wc_ref, feat_ref, *out_refs):
    xn = _rms(h_ref[...], g_ref[0:1, :]).astype(BF16)
    row_refs = out_refs[:len(ROW_SEGS)]
    col_refs = out_refs[len(ROW_SEGS):]
    off = 0
    feat_off = 0
    for ref, (n, dtype, has_feat) in zip(row_refs, ROW_SEGS):
        val = _dot(xn, wr_ref[:, off:off + n]).astype(dtype)
        if has_feat:
            val = val + feat_ref[:, feat_off:feat_off + n]
            feat_off += n
        ref[...] = val
        off += n
    off = 0
    for ref, (n, has_ones) in zip(col_refs, COL_SEGS):
        val = _dot_nt(wc_ref[off:off + n, :], xn)
        if has_ones:
            row = lax.broadcasted_iota(jnp.int32, val.shape, 0)
            is_ones = row == ONES_ROW
            for g in range(1, NSA_KV_HEADS):
                is_ones = is_ones | (row == g * V_ROWS + ONES_ROW)
            val = jnp.where(is_ones, 1.0, val)
        ref[...] = val.astype(BF16)
        off += n


def _inproj(h, gains, w_rows, w_cols, feats, layer):
    t = h.shape[0]
    tiles_per_seq = feats.shape[0] // PROJ_TM
    return pl.pallas_call(
        _inproj_body,
        grid=(t // PROJ_TM,),
        in_specs=[
            pl.BlockSpec((PROJ_TM, D_MODEL), lambda r: (r, 0)),
            pl.BlockSpec((None, 4, D_MODEL), lambda r: (layer, 0, 0)),
            _resident((D_MODEL, ROW_WIDTH), lambda r: (0, 0)),
            _resident((COL_WIDTH, D_MODEL), lambda r: (0, 0)),
            pl.BlockSpec((PROJ_TM, FEAT_WIDTH), lambda r: (lax.rem(r, tiles_per_seq), 0)),
        ],
        out_specs=([pl.BlockSpec((PROJ_TM, seg[0]), lambda r: (r, 0)) for seg in ROW_SEGS]
                   + [pl.BlockSpec((seg[0], PROJ_TM), lambda r: (0, r)) for seg in COL_SEGS]),
        out_shape=([jax.ShapeDtypeStruct((t, seg[0]), seg[1]) for seg in ROW_SEGS]
                   + [jax.ShapeDtypeStruct((seg[0], t), BF16) for seg in COL_SEGS]),
        compiler_params=_params("parallel"),
        name="inproj",
    )(h, gains, w_rows, w_cols, feats)


N_CMP_PAD = 128
CMP_HALF = CMP_BLOCK // CMP_STRIDE


def _prep_cmp(pe, w1, w2):
    pe2 = jnp.concatenate([pe, pe], axis=-1)
    w1r = w1.reshape(2, CMP_HALF, CMP_STRIDE, HEAD_DIM, CMP_HIDDEN)
    w1d = jnp.concatenate([w1r, w1r], axis=3).reshape(2, CMP_HALF, CMP_STRIDE * LANES, CMP_HIDDEN)
    w2k = jnp.stack([jnp.pad(w2[0], ((0, 0), (g * LANES, KV_PAD - g * LANES - HEAD_DIM)))
                     for g in range(NSA_KV_HEADS)])
    return pe2, w1d.astype(BF16), w2k.astype(BF16), w2[1].T.astype(BF16)


def _compress_body(kc_ref, vc_ref, pe_ref, w1_ref, w2k_ref, w2v_ref, ok_ref, ov_ref):
    lane = lax.broadcasted_iota(jnp.int32, (N_CMP_PAD, LANES), 1)
    out_k = jnp.zeros((N_CMP_PAD, KV_PAD), F32)
    for kv in range(2):
        src = (kc_ref, vc_ref)[kv]
        pieces = [src[pl.ds(l, N_CMP_PAD, stride=CMP_STRIDE), :] for l in range(CMP_STRIDE)]
        for g in range(NSA_KV_HEADS):
            in_group = (lane >= g * HEAD_DIM) & (lane < (g + 1) * HEAD_DIM)
            pre = None
            for half in range(CMP_HALF):
                xs = [jnp.where(in_group, pieces[l] + pe_ref[kv, half * CMP_STRIDE + l:half * CMP_STRIDE + l + 1, :], 0.0)
                      for l in range(CMP_STRIDE)]
                xcat = jnp.concatenate(xs, axis=-1).astype(BF16)
                part = _dot(xcat, w1_ref[kv, half])
                if half == 1:
                    part = pltpu.roll(part, N_CMP_PAD - 1, 0)
                pre = part if pre is None else pre + part
            hid = _silu(pre).astype(BF16)
            if kv == 0:
                out_k = out_k + _dot(hid, w2k_ref[g])
            else:
                ov_ref[g * HEAD_DIM:(g + 1) * HEAD_DIM, :] = _dot_nt(w2v_ref[...], hid).astype(BF16)
    ok_ref[...] = out_k.astype(BF16)


def _compress(kvc, pe2, w1d, w2k, w2v, b, s):
    return pl.pallas_call(
        _compress_body,
        grid=(b,),
        in_specs=[
            pl.BlockSpec((s, LANES), lambda r: (r, 0)),
            pl.BlockSpec((s, LANES), lambda r: (r, 1)),
            pl.BlockSpec(pe2.shape, lambda r: (0, 0, 0)),
            pl.BlockSpec(w1d.shape, lambda r: (0, 0, 0, 0)),
            pl.BlockSpec(w2k.shape, lambda r: (0, 0, 0)),
            pl.BlockSpec(w2v.shape, lambda r: (0, 0)),
        ],
        out_specs=[pl.BlockSpec((None, N_CMP_PAD, KV_PAD), lambda r: (r, 0, 0)),
                   pl.BlockSpec((None, NSA_KV_HEADS * HEAD_DIM, N_CMP_PAD), lambda r: (r, 0, 0))],
        out_shape=[jax.ShapeDtypeStruct((b, N_CMP_PAD, KV_PAD), BF16),
                   jax.ShapeDtypeStruct((b, NSA_KV_HEADS * HEAD_DIM, N_CMP_PAD), BF16)],
        compiler_params=_params("parallel"),
        name="compress",
    )(kvc, kvc, pe2, w1d, w2k, w2v)


ATT_T = 256
SEL_SHIFT = SEL_BLOCK.bit_length() - 1
MASK_DIST = 2.0 ** 100
MASK_BIAS = -(2.0 ** 100)
WIN_KEYS = WINDOW + ATT_T
DIAG_KEYS = 256
TILES_PER_DIAG = DIAG_KEYS // ATT_T
HEAD_PACK = 4


def _overlap_t(s):
    n_cmp = (s - CMP_BLOCK) // CMP_STRIDE + 1
    n_sel = s // SEL_BLOCK
    assert n_cmp <= N_CMP_PAD
    cs = np.arange(n_cmp)[None, :] * CMP_STRIDE
    ss = np.arange(n_sel)[:, None] * SEL_BLOCK
    ov = np.clip(np.minimum(cs + CMP_BLOCK, ss + SEL_BLOCK) - np.maximum(cs, ss), 0, None) / CMP_BLOCK
    return jnp.asarray(np.pad(ov, ((0, 0), (0, N_CMP_PAD - n_cmp))), BF16)


def _softmax_pv_t(sc, v_t):
    p = jnp.exp(sc - jnp.max(sc, axis=0, keepdims=True))
    o = _dot(v_t, p.astype(BF16))
    return o[0:HEAD_DIM, :] * (1.0 / o[ONES_ROW:ONES_ROW + 1, :])


def _attn_body(q_ref, kc_ref, vc_ref, ks_ref, vs_ref, kw_ref, vw_ref, gl_ref, ovt_ref, o_ref, out_ref,
               *, n_sel, n_variants):
    i = pl.program_id(1)
    t0 = i * ATT_T
    gates = jax.nn.sigmoid(gl_ref[...].T)

    def gate(hh, branch):
        return gates[3 * hh + branch:3 * hh + branch + 1, :]

    def rel(n):
        return lax.broadcasted_iota(jnp.int32, (n, ATT_T), 1) - lax.broadcasted_iota(jnp.int32, (n, ATT_T), 0)

    w0 = pl.multiple_of(jnp.maximum(t0 - WINDOW, 0), ATT_T)
    dist_w = (t0 - w0) + rel(WIN_KEYS)
    bias_w = jnp.where((dist_w >= 0) & (dist_w < WINDOW), 0.0, MASK_BIAS)

    blk_c = lax.broadcasted_iota(jnp.int32, (N_CMP_PAD, ATT_T), 0)
    d_cmp = t0 + lax.broadcasted_iota(jnp.int32, (N_CMP_PAD, ATT_T), 1) - (blk_c * CMP_STRIDE + (CMP_BLOCK - 1))
    valid_c = d_cmp >= 0
    dm_c = jnp.where(valid_c, d_cmp.astype(F32), MASK_DIST)

    blk = lax.broadcasted_iota(jnp.int32, (n_sel, ATT_T), 0)
    tq = t0 + lax.broadcasted_iota(jnp.int32, (n_sel, ATT_T), 1)
    cur = tq >> SEL_SHIFT
    forced = (blk == 0) | (blk == cur) | (blk == cur - 1)
    eligible = blk * SEL_BLOCK <= tq
    feat_row = lax.broadcasted_iota(jnp.int32, (LANES - BLOCK_LANE - n_sel, ATT_T), 0)

    def lanes(parts):
        return jnp.concatenate(parts, axis=1)

    def tiled(a):
        return lanes([a] * HEAD_PACK)

    bias_w = tiled(bias_w)
    dm_c = tiled(dm_c)
    valid_c = tiled(valid_c)
    pad = jnp.zeros((LANES - HEAD_DIM, ATT_T), BF16)

    def gate_row(heads, branch):
        return lanes([gate(hh, branch) for hh in heads])

    packs = []
    for g in range(NSA_KV_HEADS):
        group = slice(g * LANES, (g + 1) * LANES)
        group_packs = [[g * NSA_GROUP + k + h for h in range(HEAD_PACK)] for k in range(0, NSA_GROUP, HEAD_PACK)]
        slopes = [[2.0 ** -(hh + 1) for hh in heads] for heads in group_packs]
        qs = [[q_ref[hh * HEAD_DIM:(hh + 1) * HEAD_DIM, :] * (HEAD_DIM ** -0.5) for hh in heads]
              for heads in group_packs]

        kc = kc_ref[:, group]
        vc_t = vc_ref[g * HEAD_DIM:(g + 1) * HEAD_DIM, :]
        psum = jnp.zeros((N_CMP_PAD, ATT_T), F32)
        for pk, heads in enumerate(group_packs):
            slope_row = lanes([jnp.full((1, ATT_T), sl, F32) for sl in slopes[pk]])
            sc = _dot(kc, lanes([jnp.concatenate([q, pad], axis=0) for q in qs[pk]])) - slope_row * dm_c
            e = jnp.exp(sc - jnp.max(sc, axis=0, keepdims=True))
            p = e * (1.0 / jnp.sum(e, axis=0, keepdims=True))
            p = jnp.where(valid_c, p, 0.0)
            for h in range(HEAD_PACK):
                psum = psum + p[:, h * ATT_T:(h + 1) * ATT_T]
            out_ref[heads[0] // HEAD_PACK] = gate_row(heads, 0) * _dot(vc_t, p.astype(BF16))

        p_hi = psum.astype(BF16)
        p_lo = (psum - p_hi.astype(F32)).astype(BF16)
        imp = _dot(ovt_ref[...], p_hi) + _dot(ovt_ref[...], p_lo)
        imp = jnp.where(eligible, imp + jnp.where(forced, FORCE_BONUS, 0.0), -1.0)
        rank = jnp.zeros((n_sel, ATT_T), F32)
        for j in range(n_sel):
            cj = imp[j:j + 1, :]
            tie = jnp.where(blk > j, 1.0, 0.0)
            rank = rank + jnp.where(cj > imp, 1.0, jnp.where(cj == imp, tie, 0.0))
        drop = jnp.where(rank < min(SEL_TOPK, n_sel), 0.0, MASK_BIAS).astype(BF16)

        for pk, heads in enumerate(group_packs):
            cols = []
            for q, sl in zip(qs[pk], slopes[pk]):
                alibi = jnp.where(feat_row == ALIBI_LANE - BLOCK_LANE - n_sel, sl * SEL_BLOCK,
                                  jnp.where(feat_row == ALIBI_LANE + 1 - BLOCK_LANE - n_sel, sl, 0.0))
                cols.append(jnp.concatenate([q, drop, alibi.astype(BF16)], axis=0))
            packs.append((g, heads, lanes(cols)))

    def attend(branch, keys_of, values_of, masked):
        def finish(pack, sc):
            g, heads, _ = pack
            out_ref[heads[0] // HEAD_PACK] += gate_row(heads, branch) * _softmax_pv_t(masked(sc), values_of(g))

        pending = None
        for pack in packs:
            sc = _dot(keys_of(pack[0]), pack[2])
            if pending is not None:
                finish(*pending)
            pending = (pack, sc)
        finish(*pending)

    def selected(n_keys):
        causal = tiled(jnp.where(rel(DIAG_KEYS) + (t0 - (n_keys - DIAG_KEYS)) >= 0, 0.0, MASK_BIAS))

        def masked(sc):
            tail = sc[n_keys - DIAG_KEYS:, :] + causal
            return tail if n_keys == DIAG_KEYS else jnp.concatenate([sc[:n_keys - DIAG_KEYS, :], tail], axis=0)

        attend(1, lambda g: ks_ref[0:n_keys, g * LANES:(g + 1) * LANES],
               lambda g: vs_ref[g * V_ROWS:(g + 1) * V_ROWS, 0:n_keys], masked)

    for v in range(n_variants):
        pl.when(lax.div(i, TILES_PER_DIAG) == v)(functools.partial(selected, (v + 1) * DIAG_KEYS))

    attend(2, lambda g: kw_ref[pl.ds(w0, WIN_KEYS), g * LANES:(g + 1) * LANES],
           lambda g: vw_ref[g * V_ROWS:(g + 1) * V_ROWS, pl.ds(w0, WIN_KEYS)], lambda sc: sc + bias_w)

    o_t = [out_ref[hh // HEAD_PACK][:, (hh % HEAD_PACK) * ATT_T:(hh % HEAD_PACK + 1) * ATT_T]
           for hh in range(NSA_HEADS)]
    o_ref[...] = jnp.concatenate(o_t, axis=0).T.astype(o_ref.dtype)


def _attention(q_t, kcb, vcb_t, ks, vs_t, kw, vw_t, gldt, ovt, b, s):
    nq = s // ATT_T
    n_sel = s // SEL_BLOCK
    assert DIAG_KEYS % ATT_T == 0 and s % DIAG_KEYS == 0 and WINDOW % ATT_T == 0
    keys = pl.BlockSpec((s, KV_PAD), lambda r, i: (r, 0))
    vals_t = pl.BlockSpec((NSA_KV_HEADS * V_ROWS, s), lambda r, i: (0, r))
    return pl.pallas_call(
        functools.partial(_attn_body, n_sel=n_sel, n_variants=s // DIAG_KEYS),
        grid=(b, nq),
        in_specs=[
            pl.BlockSpec((NSA_HEADS * HEAD_DIM, ATT_T), lambda r, i: (0, r * nq + i)),
            pl.BlockSpec((None, N_CMP_PAD, KV_PAD), lambda r, i: (r, 0, 0)),
            pl.BlockSpec((None, NSA_KV_HEADS * HEAD_DIM, N_CMP_PAD), lambda r, i: (r, 0, 0)),
            keys, vals_t, keys, vals_t,
            pl.BlockSpec((ATT_T, LANES), lambda r, i: (r * nq + i, 0)),
            pl.BlockSpec(ovt.shape, lambda r, i: (0, 0)),
        ],
        out_specs=pl.BlockSpec((ATT_T, NSA_HEADS * HEAD_DIM), lambda r, i: (r * nq + i, 0)),
        out_shape=jax.ShapeDtypeStruct((b * s, NSA_HEADS * HEAD_DIM), BF16),
        scratch_shapes=[pltpu.VMEM((NSA_HEADS // HEAD_PACK, HEAD_DIM, HEAD_PACK * ATT_T), F32)],
        compiler_params=_params("parallel", "parallel"),
        name="nsa_attention",
    )(q_t, kcb, vcb_t, ks, vs_t, kw, vw_t, gldt, ovt)


CONV_PAD = 8
DT_LANE = 3 * NSA_HEADS


def _split_dot(a, b):
    b0 = b.astype(BF16)
    r1 = b - b0.astype(F32)
    b1 = r1.astype(BF16)
    b2 = (r1 - b1.astype(F32)).astype(BF16)
    return _dot(a, b0) + _dot(a, b1) + _dot(a, b2)


def _split_dot_rhs(a, b):
    a0 = a.astype(BF16)
    r1 = a - a0.astype(F32)
    a1 = r1.astype(BF16)
    a2 = (r1 - a1.astype(F32)).astype(BF16)
    return _dot(a0, b) + _dot(a1, b) + _dot(a2, b)


def _softplus(x):
    return jnp.maximum(x, 0.0) + jnp.log1p(jnp.exp(-jnp.abs(x)))


def _ssd_consts(dt_bias, a_log):
    lanes = np.arange(DT_LANE, DT_LANE + SSM_HEADS)
    dtb = jnp.zeros((1, LANES), F32).at[0, lanes].set(dt_bias)
    alog = jnp.zeros((1, LANES), F32).at[0, lanes].set(a_log)
    expand = np.zeros((LANES, SSM_D_INNER), np.float32)
    for h in range(SSM_HEADS):
        expand[DT_LANE + h, h * SSM_HEAD_DIM:(h + 1) * SSM_HEAD_DIM] = 1.0
    return dtb, alog, jnp.asarray(expand, BF16)


SSD_CHUNKS = 8
SSD_ROWS = SSD_CHUNKS * SSM_CHUNK


def _ssd_body(xbc_ref, z_ref, gldt_ref, cw_ref, cb_ref, dtb_ref, alog_ref, expand_ref, d_ref, nw_ref, o_ref,
              state_ref, xe_ref):
    step = pl.program_id(1)
    L = SSM_CHUNK

    @pl.when(step == 0)
    def _():
        state_ref[...] = jnp.zeros(state_ref.shape, F32)
        xe_ref[0:CONV_PAD, :] = jnp.zeros((CONV_PAD, SSM_CONV_DIM), F32)

    x_raw = xbc_ref[...]
    xe_ref[CONV_PAD:CONV_PAD + SSD_ROWS, :] = x_raw
    acc = jnp.zeros((SSD_ROWS, SSM_CONV_DIM), F32)
    for k in range(SSM_CONV):
        lo = CONV_PAD - (SSM_CONV - 1) + k
        acc = acc + xe_ref[lo:lo + SSD_ROWS, :] * cw_ref[k:k + 1, :]
    xe_ref[0:CONV_PAD, :] = x_raw[SSD_ROWS - CONV_PAD:, :]
    xa = _silu(acc + cb_ref[...])

    row = lax.broadcasted_iota(jnp.int32, (L, L), 0)
    col = lax.broadcasted_iota(jnp.int32, (L, L), 1)
    causal = col <= row
    tri = jnp.where(causal, 1.0, 0.0).astype(BF16)
    lane = lax.broadcasted_iota(jnp.int32, (L, LANES), 1)
    rpg = SSM_HEADS // SSM_GROUPS
    gw = rpg * SSM_HEAD_DIM

    lane128 = lax.broadcasted_iota(jnp.int32, (1, LANES), 1)
    on_dt = (lane128 >= DT_LANE) & (lane128 < DT_LANE + SSM_HEADS)
    dt_all = _softplus(gldt_ref[...] + dtb_ref[...])
    a_all = dt_all * jnp.where(on_dt, -jnp.exp(alog_ref[...]), 0.0)
    z_act = _silu(z_ref[...])

    for j in range(SSD_CHUNKS):
        rows = slice(j * L, (j + 1) * L)
        xs = xa[rows, :SSM_D_INNER]
        bm = xa[rows, SSM_D_INNER:SSM_D_INNER + SSM_BC_DIM]
        cm = xa[rows, SSM_D_INNER + SSM_BC_DIM:]
        a_cs_c = _split_dot(tri, a_all[rows, :])
        a_cs_t = a_cs_c.T[DT_LANE:DT_LANE + SSM_HEADS, :]
        dt = _split_dot_rhs(dt_all[rows, :], expand_ref[...])
        a_cs = _split_dot_rhs(a_cs_c, expand_ref[...])
        a_last = a_cs[L - 1:L, :]

        xdt = xs * dt
        xw = (xdt * jnp.exp(a_last - a_cs)).astype(BF16)
        exp_cs = jnp.exp(a_cs)
        ys = []
        for g in range(SSM_GROUPS):
            bm_g = bm[:, g * SSM_STATE:(g + 1) * SSM_STATE]
            cm_g = cm[:, g * SSM_STATE:(g + 1) * SSM_STATE].astype(BF16)
            cb = _dot_nt(cm_g, bm_g.astype(BF16))
            prev = state_ref[:, g * gw:(g + 1) * gw]
            y_off = _dot(cm_g, prev.astype(BF16)) * exp_cs[:, g * gw:(g + 1) * gw]
            new_state = _dot(bm_g.T.astype(BF16), xw[:, g * gw:(g + 1) * gw])
            state_ref[:, g * gw:(g + 1) * gw] = prev * jnp.exp(a_last[:, g * gw:(g + 1) * gw]) + new_state
            for pair in range(rpg // 2):
                h0 = g * rpg + 2 * pair
                x_pair = xdt[:, h0 * SSM_HEAD_DIM:(h0 + 2) * SSM_HEAD_DIM].astype(BF16)
                halves = []
                for h in (h0, h0 + 1):
                    diff = a_cs[:, h * SSM_HEAD_DIM:h * SSM_HEAD_DIM + 1] - a_cs_t[h:h + 1, :]
                    decay = jnp.exp(jnp.where(causal, diff, NEG_INF))
                    halves.append(_dot((cb * decay).astype(BF16), x_pair))
                ys.append(jnp.where(lane < SSM_HEAD_DIM, halves[0], halves[1])
                          + y_off[:, 2 * pair * SSM_HEAD_DIM:(2 * pair + 2) * SSM_HEAD_DIM])
        y = jnp.concatenate(ys, axis=-1) + xs * d_ref[...]
        y = y * z_act[rows, :]
        outs = []
        for g in range(SSM_GROUPS):
            yg = y[:, g * gw:(g + 1) * gw]
            outs.append(_rms(yg, nw_ref[:, g * gw:(g + 1) * gw]))
        o_ref[rows, :] = jnp.concatenate(outs, axis=-1).astype(o_ref.dtype)


def _ssd(xbc, z, gldt, conv_w, conv_b, dt_bias, a_log, d_skip, norm_w, b, s):
    nc = s // SSD_ROWS
    rows = lambda n: pl.BlockSpec((SSD_ROWS, n), lambda r, c: (r * nc + c, 0))
    full = lambda arr: pl.BlockSpec(arr.shape, lambda r, c: (0,) * arr.ndim)
    consts = (conv_w, conv_b[None, :], *_ssd_consts(dt_bias, a_log), jnp.repeat(d_skip, SSM_HEAD_DIM)[None, :],
              norm_w[None, :])
    return pl.pallas_call(
        _ssd_body,
        grid=(b, nc),
        in_specs=[rows(SSM_CONV_DIM), rows(SSM_D_INNER), rows(LANES)] + [full(a) for a in consts],
        out_specs=rows(SSM_D_INNER),
        out_shape=jax.ShapeDtypeStruct((b * s, SSM_D_INNER), BF16),
        scratch_shapes=[
            pltpu.VMEM((SSM_STATE, SSM_D_INNER), F32),
            pltpu.VMEM((CONV_PAD + SSD_ROWS, SSM_CONV_DIM), F32),
        ],
        compiler_params=_params("parallel", "arbitrary"),
        name="ssd",
    )(xbc, z, gldt, *consts)


def _even_mixer(h, gains, layer, w_in, pe, w1, w2, conv_w, conv_b, dt_bias, a_log, d_skip, norm_w, b, s):
    w_rows, w_cols = _prep_w_in(w_in)
    kvc, ks, kw, z, xbc, gldt, q_t, vs_t, vw_t = _inproj(h, gains, w_rows, w_cols, _position_features(s), layer)
    kcb, vcb_t = _compress(kvc, *_prep_cmp(pe, w1, w2), b, s)
    o_attn = _attention(q_t, kcb, vcb_t, ks, vs_t, kw, vw_t, gldt, _overlap_t(s), b, s)
    o_ssm = _ssd(xbc, z, gldt, conv_w, conv_b, dt_bias, a_log, d_skip, norm_w, b, s)
    return o_attn, o_ssm


def kernel(x, norm_gains, ffn_w_gate, ffn_w_up, ffn_w_down, ev_w_in, ev_w_out, nsa_cmp_pe, nsa_cmp_w1, nsa_cmp_w2,
           ssm_conv_w, ssm_conv_b, ssm_dt_bias, ssm_a_log, ssm_d, ssm_norm_w, od_w_in, od_ln_w, od_ln_b, od_w_s,
           od_b_s, od_w_out):
    b, s, d = x.shape
    depth = norm_gains.shape[0]
    h = x.reshape(b * s, d)
    wg = ffn_w_gate.astype(BF16)
    wu = ffn_w_up.astype(BF16)
    wd = ffn_w_down.astype(BF16)
    od_in = od_w_in.astype(BF16)
    od_out = od_w_out.astype(BF16)
    od_lnw = od_ln_w[:, None, :]
    od_lnb = od_ln_b[:, None, :]
    od_bst = jnp.swapaxes(od_b_s, 1, 2)
    for layer in range(depth):
        i = layer // 2
        if layer % 2 == 0:
            o_attn, o_ssm = _even_mixer(h, norm_gains, layer, ev_w_in[i], nsa_cmp_pe[i], nsa_cmp_w1[i],
                                        nsa_cmp_w2[i], ssm_conv_w[i], ssm_conv_b[i], ssm_dt_bias[i], ssm_a_log[i],
                                        ssm_d[i], ssm_norm_w[i], b, s)
            h = _ffn(h, norm_gains, wg, wu, wd, layer, (o_attn, o_ssm, ev_w_out[i].astype(BF16)))
        else:
            h = _gmlp(h, norm_gains, od_in, od_lnw, od_lnb, od_w_s, od_bst, od_out, layer, i)
            h = _ffn(h, norm_gains, wg, wu, wd, layer)
    return h.reshape(b, s, d)
```

```python
import functools

import numpy as np
import jax
import jax.numpy as jnp
from jax import lax
from jax.experimental import pallas as pl
from jax.experimental.pallas import tpu as pltpu

F32 = jnp.float32
BF16 = jnp.bfloat16

D_MODEL = 1024
RMS_EPS = 1e-6
LN_EPS = 1e-5
NEG_INF = -1e30

NSA_HEADS = 8
NSA_KV_HEADS = 2
NSA_GROUP = NSA_HEADS // NSA_KV_HEADS
HEAD_DIM = 64
CMP_BLOCK = 32
CMP_STRIDE = 16
CMP_HIDDEN = 256
SEL_BLOCK = 64
SEL_TOPK = 16
WINDOW = 512
FORCE_BONUS = 1e4

SSM_HEADS = 8
SSM_HEAD_DIM = 64
SSM_D_INNER = SSM_HEADS * SSM_HEAD_DIM
SSM_GROUPS = 2
SSM_STATE = 128
SSM_CONV = 4
SSM_CHUNK = 128
SSM_BC_DIM = SSM_GROUPS * SSM_STATE
SSM_CONV_DIM = SSM_D_INNER + 2 * SSM_BC_DIM

GMLP_WIDTH = 2 * D_MODEL
GMLP_GROUPS = 8
GMLP_GROUP_DIM = GMLP_WIDTH // GMLP_GROUPS
GMLP_CHUNK = 128

LANES = 128
VMEM_LIMIT = 56 * 1024 * 1024


def _params(*sem):
    return pltpu.CompilerParams(dimension_semantics=sem, vmem_limit_bytes=VMEM_LIMIT)


def _rms(x, g):
    return x * lax.rsqrt(jnp.mean(x * x, axis=-1, keepdims=True) + RMS_EPS) * g


def _silu(x):
    return x * jax.nn.sigmoid(x)


def _dot(a, b):
    return jnp.dot(a, b, preferred_element_type=F32)


def _dot_nt(a, b):
    return lax.dot_general(a, b, (((1,), (1,)), ((), ())), preferred_element_type=F32)


def _resident(block_shape, index_map):
    return pl.BlockSpec(block_shape, index_map, pipeline_mode=pl.Buffered(1))


FFN_TM = 512
FFN_CHUNK = 256


def _ffn_body(h_ref, g_ref, wg_ref, wu_ref, wd_ref, *rest):
    o_ref = rest[-1]
    x = h_ref[...]
    if len(rest) > 1:
        oa_ref, os_ref, wo_ref = rest[:3]
        na = oa_ref.shape[-1]
        x = x + _rms(_dot(oa_ref[...], wo_ref[:na, :]) + _dot(os_ref[...], wo_ref[na:, :]), g_ref[1:2, :])
    xn = _rms(x, g_ref[2:3, :]).astype(BF16)
    acc = jnp.zeros(x.shape, F32)
    for c in range(wg_ref.shape[1] // FFN_CHUNK):
        cols = slice(c * FFN_CHUNK, (c + 1) * FFN_CHUNK)
        a = _dot(xn, wg_ref[:, cols])
        b = _dot(xn, wu_ref[:, cols])
        acc = acc + _dot((_silu(a) * b).astype(BF16), wd_ref[cols, :])
    o_ref[...] = x + _rms(acc, g_ref[3:4, :])


def _ffn(h, gains, wg, wu, wd, layer, mixer_out=None):
    t = h.shape[0]
    hidden = wg.shape[2]
    assert hidden % FFN_CHUNK == 0 and t % FFN_TM == 0
    rows = lambda n: pl.BlockSpec((FFN_TM, n), lambda i: (i, 0))
    in_specs = [
        rows(D_MODEL),
        pl.BlockSpec((None, 4, D_MODEL), lambda i: (layer, 0, 0)),
        _resident((None, D_MODEL, hidden), lambda i: (layer, 0, 0)),
        _resident((None, D_MODEL, hidden), lambda i: (layer, 0, 0)),
        _resident((None, hidden, D_MODEL), lambda i: (layer, 0, 0)),
    ]
    args = [h, gains, wg, wu, wd]
    if mixer_out is not None:
        oa, os_, w_out = mixer_out
        in_specs += [rows(oa.shape[1]), rows(os_.shape[1]), _resident(w_out.shape, lambda i: (0, 0))]
        args += [oa, os_, w_out]
    return pl.pallas_call(
        _ffn_body,
        grid=(t // FFN_TM,),
        in_specs=in_specs,
        out_specs=rows(D_MODEL),
        out_shape=jax.ShapeDtypeStruct((t, D_MODEL), F32),
        compiler_params=_params("parallel"),
        name="ffn",
    )(*args)


GMLP_TM = 512


def _gelu(x):
    return 0.5 * x * (1.0 + lax.erf(x * np.float32(1.0 / np.sqrt(2.0))))


GMLP_COLS = 1024


def _gmlp_body(h_ref, g_ref, win_ref, lnw_ref, lnb_ref, ws_ref, bs_ref, wout_ref, o_ref):
    x = h_ref[...]
    xn = _rms(x, g_ref[0:1, :]).astype(BF16)
    v = _gelu(_dot(xn, win_ref[:, GMLP_WIDTH:]))
    mu = jnp.mean(v, axis=-1, keepdims=True)
    vc = v - mu
    var = jnp.mean(vc * vc, axis=-1, keepdims=True)
    v = (vc * lax.rsqrt(var + LN_EPS) * lnw_ref[...] + lnb_ref[...]).astype(BF16)
    row = lax.broadcasted_iota(jnp.int32, (GMLP_CHUNK, GMLP_CHUNK), 0)
    col = lax.broadcasted_iota(jnp.int32, (GMLP_CHUNK, GMLP_CHUNK), 1)
    causal = col <= row
    y = jnp.zeros(x.shape, F32)
    for c0 in range(0, GMLP_WIDTH, GMLP_COLS):
        u = _gelu(_dot(xn, win_ref[:, c0:c0 + GMLP_COLS]))
        mixed = []
        for g in range(c0 // GMLP_GROUP_DIM, (c0 + GMLP_COLS) // GMLP_GROUP_DIM):
            ws = jnp.where(causal, ws_ref[g], 0.0).astype(BF16)
            bias = bs_ref[:, g:g + 1]
            cols = slice(g * GMLP_GROUP_DIM, (g + 1) * GMLP_GROUP_DIM)
            mixed.append(jnp.concatenate(
                [_dot(ws, v[c * GMLP_CHUNK:(c + 1) * GMLP_CHUNK, cols]) + bias for c in range(GMLP_TM // GMLP_CHUNK)],
                axis=0))
        gate = (u * jnp.concatenate(mixed, axis=1)).astype(BF16)
        y = y + _dot(gate, wout_ref[c0:c0 + GMLP_COLS, :])
    o_ref[...] = x + _rms(y, g_ref[1:2, :])


def _gmlp(h, gains, w_in, ln_w, ln_b, w_s, b_s_t, w_out, layer, i):
    t = h.shape[0]
    const2 = lambda r: (i, 0, 0)
    return pl.pallas_call(
        _gmlp_body,
        grid=(t // GMLP_TM,),
        in_specs=[
            pl.BlockSpec((GMLP_TM, D_MODEL), lambda r: (r, 0)),
            pl.BlockSpec((None, 4, D_MODEL), lambda r: (layer, 0, 0)),
            _resident((None, D_MODEL, 2 * GMLP_WIDTH), const2),
            pl.BlockSpec((None, 1, GMLP_WIDTH), const2),
            pl.BlockSpec((None, 1, GMLP_WIDTH), const2),
            pl.BlockSpec((None, GMLP_GROUPS, GMLP_CHUNK, GMLP_CHUNK), lambda r: (i, 0, 0, 0)),
            pl.BlockSpec((None, GMLP_CHUNK, GMLP_GROUPS), const2),
            _resident((None, GMLP_WIDTH, D_MODEL), const2),
        ],
        out_specs=pl.BlockSpec((GMLP_TM, D_MODEL), lambda r: (r, 0)),
        out_shape=jax.ShapeDtypeStruct((t, D_MODEL), F32),
        compiler_params=_params("parallel"),
        name="gmlp",
    )(h, gains, w_in, ln_w, ln_b, w_s, b_s_t, w_out)


PROJ_TM = 512
KV_PAD = NSA_KV_HEADS * LANES
BLOCK_LANE = HEAD_DIM
ALIBI_LANE = 96
V_ROWS = HEAD_DIM + 16
ONES_ROW = HEAD_DIM
ROW_SEGS = ((2 * LANES, F32, False),
            (KV_PAD, BF16, True),
            (KV_PAD, BF16, True),
            (SSM_D_INNER, F32, False),
            (SSM_CONV_DIM, F32, False),
            (LANES, F32, False))
COL_SEGS = ((NSA_HEADS * HEAD_DIM, False),
            (NSA_KV_HEADS * V_ROWS, True),
            (NSA_KV_HEADS * V_ROWS, True))
ROW_WIDTH = sum(seg[0] for seg in ROW_SEGS)
COL_WIDTH = sum(seg[0] for seg in COL_SEGS)
FEAT_WIDTH = sum(seg[0] for seg in ROW_SEGS if seg[2])


def _prep_w_in(w):
    nq = NSA_HEADS * HEAD_DIM
    nkv = NSA_KV_HEADS * HEAD_DIM

    def pad_groups(a, width):
        a = a.reshape(D_MODEL, NSA_KV_HEADS, HEAD_DIM)
        return jnp.pad(a, ((0, 0), (0, 0), (0, width - HEAD_DIM))).reshape(D_MODEL, NSA_KV_HEADS * width)

    o = nq
    kc, vc, ks, vs, kw, vw = [w[:, o + j * nkv:o + (j + 1) * nkv] for j in range(6)]
    o += 6 * nkv
    gl = w[:, o:o + 3 * NSA_HEADS]
    o += 3 * NSA_HEADS
    z = w[:, o:o + SSM_D_INNER]
    o += SSM_D_INNER
    xbc = w[:, o:o + SSM_CONV_DIM]
    o += SSM_CONV_DIM
    dt = w[:, o:o + SSM_HEADS]
    gldt = jnp.concatenate([gl, dt, jnp.zeros((D_MODEL, LANES - 3 * NSA_HEADS - SSM_HEADS), w.dtype)], axis=1)
    rows = jnp.concatenate([kc, vc, pad_groups(ks, LANES), pad_groups(kw, LANES), z, xbc, gldt], axis=1)
    cols = jnp.concatenate([w[:, :nq], pad_groups(vs, V_ROWS), pad_groups(vw, V_ROWS)], axis=1).T
    assert rows.shape[1] == ROW_WIDTH and cols.shape[0] == COL_WIDTH
    return rows.astype(BF16), cols.astype(BF16)


def _position_features(s):
    n_sel = s // SEL_BLOCK
    assert BLOCK_LANE + n_sel <= ALIBI_LANE and ALIBI_LANE + 2 <= LANES
    pos = np.arange(s)
    alibi = np.zeros((s, LANES), np.float32)
    alibi[:, ALIBI_LANE] = pos // SEL_BLOCK
    alibi[:, ALIBI_LANE + 1] = pos % SEL_BLOCK
    k_sel = alibi.copy()
    k_sel[pos, BLOCK_LANE + pos // SEL_BLOCK] = 1.0
    feats = [np.tile(a, (1, NSA_KV_HEADS)) for a in (k_sel, alibi)]
    return jnp.asarray(np.concatenate(feats, axis=1), BF16)


def _inproj_body(h_ref, g_ref, wr_ref, wc_ref, feat_ref, *out_refs):
    xn = _rms(h_ref[...], g_ref[0:1, :]).astype(BF16)
    row_refs = out_refs[:len(ROW_SEGS)]
    col_refs = out_refs[len(ROW_SEGS):]
    off = 0
    feat_off = 0
    for ref, (n, dtype, has_feat) in zip(row_refs, ROW_SEGS):
        val = _dot(xn, wr_ref[:, off:off + n]).astype(dtype)
        if has_feat:
            val = val + feat_ref[:, feat_off:feat_off + n]
            feat_off += n
        ref[...] = val
        off += n
    off = 0
    for ref, (n, has_ones) in zip(col_refs, COL_SEGS):
        val = _dot_nt(wc_ref[off:off + n, :], xn)
        if has_ones:
            row = lax.broadcasted_iota(jnp.int32, val.shape, 0)
            is_ones = row == ONES_ROW
            for g in range(1, NSA_KV_HEADS):
                is_ones = is_ones | (row == g * V_ROWS + ONES_ROW)
            val = jnp.where(is_ones, 1.0, val)
        ref[...] = val.astype(BF16)
        off += n


def _inproj(h, gains, w_rows, w_cols, feats, layer):
    t = h.shape[0]
    tiles_per_seq = feats.shape[0] // PROJ_TM
    return pl.pallas_call(
        _inproj_body,
        grid=(t // PROJ_TM,),
        in_specs=[
            pl.BlockSpec((PROJ_TM, D_MODEL), lambda r: (r, 0)),
            pl.BlockSpec((None, 4, D_MODEL), lambda r: (layer, 0, 0)),
            _resident((D_MODEL, ROW_WIDTH), lambda r: (0, 0)),
            _resident((COL_WIDTH, D_MODEL), lambda r: (0, 0)),
            pl.BlockSpec((PROJ_TM, FEAT_WIDTH), lambda r: (lax.rem(r, tiles_per_seq), 0)),
        ],
        out_specs=([pl.BlockSpec((PROJ_TM, seg[0]), lambda r: (r, 0)) for seg in ROW_SEGS]
                   + [pl.BlockSpec((seg[0], PROJ_TM), lambda r: (0, r)) for seg in COL_SEGS]),
        out_shape=([jax.ShapeDtypeStruct((t, seg[0]), seg[1]) for seg in ROW_SEGS]
                   + [jax.ShapeDtypeStruct((seg[0], t), BF16) for seg in COL_SEGS]),
        compiler_params=_params("parallel"),
        name="inproj",
    )(h, gains, w_rows, w_cols, feats)


N_CMP_PAD = 128
CMP_HALF = CMP_BLOCK // CMP_STRIDE


def _prep_cmp(pe, w1, w2):
    pe2 = jnp.concatenate([pe, pe], axis=-1)
    w1r = w1.reshape(2, CMP_HALF, CMP_STRIDE, HEAD_DIM, CMP_HIDDEN)
    w1d = jnp.concatenate([w1r, w1r], axis=3).reshape(2, CMP_HALF, CMP_STRIDE * LANES, CMP_HIDDEN)
    w2k = jnp.stack([jnp.pad(w2[0], ((0, 0), (g * LANES, KV_PAD - g * LANES - HEAD_DIM)))
                     for g in range(NSA_KV_HEADS)])
    return pe2, w1d.astype(BF16), w2k.astype(BF16), w2[1].T.astype(BF16)


def _compress_body(kc_ref, vc_ref, pe_ref, w1_ref, w2k_ref, w2v_ref, ok_ref, ov_ref):
    lane = lax.broadcasted_iota(jnp.int32, (N_CMP_PAD, LANES), 1)
    out_k = jnp.zeros((N_CMP_PAD, KV_PAD), F32)
    for kv in range(2):
        src = (kc_ref, vc_ref)[kv]
        pieces = [src[pl.ds(l, N_CMP_PAD, stride=CMP_STRIDE), :] for l in range(CMP_STRIDE)]
        for g in range(NSA_KV_HEADS):
            in_group = (lane >= g * HEAD_DIM) & (lane < (g + 1) * HEAD_DIM)
            pre = None
            for half in range(CMP_HALF):
                xs = [jnp.where(in_group, pieces[l] + pe_ref[kv, half * CMP_STRIDE + l:half * CMP_STRIDE + l + 1, :], 0.0)
                      for l in range(CMP_STRIDE)]
                xcat = jnp.concatenate(xs, axis=-1).astype(BF16)
                part = _dot(xcat, w1_ref[kv, half])
                if half == 1:
                    part = pltpu.roll(part, N_CMP_PAD - 1, 0)
                pre = part if pre is None else pre + part
            hid = _silu(pre).astype(BF16)
            if kv == 0:
                out_k = out_k + _dot(hid, w2k_ref[g])
            else:
                ov_ref[g * HEAD_DIM:(g + 1) * HEAD_DIM, :] = _dot_nt(w2v_ref[...], hid).astype(BF16)
    ok_ref[...] = out_k.astype(BF16)


def _compress(kvc, pe2, w1d, w2k, w2v, b, s):
    return pl.pallas_call(
        _compress_body,
        grid=(b,),
        in_specs=[
            pl.BlockSpec((s, LANES), lambda r: (r, 0)),
            pl.BlockSpec((s, LANES), lambda r: (r, 1)),
            pl.BlockSpec(pe2.shape, lambda r: (0, 0, 0)),
            pl.BlockSpec(w1d.shape, lambda r: (0, 0, 0, 0)),
            pl.BlockSpec(w2k.shape, lambda r: (0, 0, 0)),
            pl.BlockSpec(w2v.shape, lambda r: (0, 0)),
        ],
        out_specs=[pl.BlockSpec((None, N_CMP_PAD, KV_PAD), lambda r: (r, 0, 0)),
                   pl.BlockSpec((None, NSA_KV_HEADS * HEAD_DIM, N_CMP_PAD), lambda r: (r, 0, 0))],
        out_shape=[jax.ShapeDtypeStruct((b, N_CMP_PAD, KV_PAD), BF16),
                   jax.ShapeDtypeStruct((b, NSA_KV_HEADS * HEAD_DIM, N_CMP_PAD), BF16)],
        compiler_params=_params("parallel"),
        name="compress",
    )(kvc, kvc, pe2, w1d, w2k, w2v)


ATT_T = 256
SEL_SHIFT = SEL_BLOCK.bit_length() - 1
MASK_DIST = 2.0 ** 100
MASK_BIAS = -(2.0 ** 100)
WIN_KEYS = WINDOW + ATT_T
DIAG_KEYS = 256
TILES_PER_DIAG = DIAG_KEYS // ATT_T
HEAD_PACK = 4


def _overlap_t(s):
    n_cmp = (s - CMP_BLOCK) // CMP_STRIDE + 1
    n_sel = s // SEL_BLOCK
    assert n_cmp <= N_CMP_PAD
    cs = np.arange(n_cmp)[None, :] * CMP_STRIDE
    ss = np.arange(n_sel)[:, None] * SEL_BLOCK
    ov = np.clip(np.minimum(cs + CMP_BLOCK, ss + SEL_BLOCK) - np.maximum(cs, ss), 0, None) / CMP_BLOCK
    return jnp.asarray(np.pad(ov, ((0, 0), (0, N_CMP_PAD - n_cmp))), BF16)


def _softmax_pv_t(sc, v_t):
    p = jnp.exp(sc - jnp.max(sc, axis=0, keepdims=True))
    o = _dot(v_t, p.astype(BF16))
    return o[0:HEAD_DIM, :] * (1.0 / o[ONES_ROW:ONES_ROW + 1, :])


def _attn_body(q_ref, kc_ref, vc_ref, ks_ref, vs_ref, kw_ref, vw_ref, gl_ref, ovt_ref, o_ref, out_ref,
               *, n_sel, n_variants):
    i = pl.program_id(1)
    t0 = i * ATT_T
    gates = jax.nn.sigmoid(gl_ref[...].T)

    def gate(hh, branch):
        return gates[3 * hh + branch:3 * hh + branch + 1, :]

    def rel(n):
        return lax.broadcasted_iota(jnp.int32, (n, ATT_T), 1) - lax.broadcasted_iota(jnp.int32, (n, ATT_T), 0)

    w0 = pl.multiple_of(jnp.maximum(t0 - WINDOW, 0), ATT_T)
    dist_w = (t0 - w0) + rel(WIN_KEYS)
    bias_w = jnp.where((dist_w >= 0) & (dist_w < WINDOW), 0.0, MASK_BIAS)

    blk_c = lax.broadcasted_iota(jnp.int32, (N_CMP_PAD, ATT_T), 0)
    d_cmp = t0 + lax.broadcasted_iota(jnp.int32, (N_CMP_PAD, ATT_T), 1) - (blk_c * CMP_STRIDE + (CMP_BLOCK - 1))
    valid_c = d_cmp >= 0
    dm_c = jnp.where(valid_c, d_cmp.astype(F32), MASK_DIST)

    blk = lax.broadcasted_iota(jnp.int32, (n_sel, ATT_T), 0)
    tq = t0 + lax.broadcasted_iota(jnp.int32, (n_sel, ATT_T), 1)
    cur = tq >> SEL_SHIFT
    forced = (blk == 0) | (blk == cur) | (blk == cur - 1)
    eligible = blk * SEL_BLOCK <= tq
    feat_row = lax.broadcasted_iota(jnp.int32, (LANES - BLOCK_LANE - n_sel, ATT_T), 0)

    def lanes(parts):
        return jnp.concatenate(parts, axis=1)

    def tiled(a):
        return lanes([a] * HEAD_PACK)

    def add_per_head(sc, bias):
        return lanes([sc[:, h * ATT_T:(h + 1) * ATT_T] + bias for h in range(HEAD_PACK)])

    dm_c = tiled(dm_c)
    valid_c = tiled(valid_c)
    pad = jnp.zeros((LANES - HEAD_DIM, ATT_T), BF16)

    def gate_row(heads, branch):
        return lanes([gate(hh, branch) for hh in heads])

    packs = []
    for g in range(NSA_KV_HEADS):
        group = slice(g * LANES, (g + 1) * LANES)
        group_packs = [[g * NSA_GROUP + k + h for h in range(HEAD_PACK)] for k in range(0, NSA_GROUP, HEAD_PACK)]
        slopes = [[2.0 ** -(hh + 1) for hh in heads] for heads in group_packs]
        qs = [[q_ref[hh * HEAD_DIM:(hh + 1) * HEAD_DIM, :] * (HEAD_DIM ** -0.5) for hh in heads]
              for heads in group_packs]

        kc = kc_ref[:, group]
        vc_t = vc_ref[g * HEAD_DIM:(g + 1) * HEAD_DIM, :]
        psum = jnp.zeros((N_CMP_PAD, ATT_T), F32)
        for pk, heads in enumerate(group_packs):
            slope_row = lanes([jnp.full((1, ATT_T), sl, F32) for sl in slopes[pk]])
            sc = _dot(kc, lanes([jnp.concatenate([q, pad], axis=0) for q in qs[pk]])) - slope_row * dm_c
            e = jnp.exp(sc - jnp.max(sc, axis=0, keepdims=True))
            p = e * (1.0 / jnp.sum(e, axis=0, keepdims=True))
            p = jnp.where(valid_c, p, 0.0)
            for h in range(HEAD_PACK):
                psum = psum + p[:, h * ATT_T:(h + 1) * ATT_T]
            out_ref[heads[0] // HEAD_PACK] = gate_row(heads, 0) * _dot(vc_t, p.astype(BF16))

        p_hi = psum.astype(BF16)
        p_lo = (psum - p_hi.astype(F32)).astype(BF16)
        imp = _dot(ovt_ref[...], p_hi) + _dot(ovt_ref[...], p_lo)
        imp = jnp.where(eligible, imp + jnp.where(forced, FORCE_BONUS, 0.0), -1.0)
        rank = jnp.zeros((n_sel, ATT_T), F32)
        for j in range(n_sel):
            cj = imp[j:j + 1, :]
            tie = jnp.where(blk > j, 1.0, 0.0)
            rank = rank + jnp.where(cj > imp, 1.0, jnp.where(cj == imp, tie, 0.0))
        drop = jnp.where(rank < min(SEL_TOPK, n_sel), 0.0, MASK_BIAS).astype(BF16)

        for pk, heads in enumerate(group_packs):
            cols = []
            for q, sl in zip(qs[pk], slopes[pk]):
                alibi = jnp.where(feat_row == ALIBI_LANE - BLOCK_LANE - n_sel, sl * SEL_BLOCK,
                                  jnp.where(feat_row == ALIBI_LANE + 1 - BLOCK_LANE - n_sel, sl, 0.0))
                cols.append(jnp.concatenate([q, drop, alibi.astype(BF16)], axis=0))
            packs.append((g, heads, lanes(cols)))

    def attend(branch, keys_of, values_of, masked):
        def finish(pack, sc):
            g, heads, _ = pack
            out_ref[heads[0] // HEAD_PACK] += gate_row(heads, branch) * _softmax_pv_t(masked(sc), values_of(g))

        pending = None
        for pack in packs:
            sc = _dot(keys_of(pack[0]), pack[2])
            if pending is not None:
                finish(*pending)
            pending = (pack, sc)
        finish(*pending)

    def selected(n_keys):
        causal = jnp.where(rel(DIAG_KEYS) + (t0 - (n_keys - DIAG_KEYS)) >= 0, 0.0, MASK_BIAS)

        def masked(sc):
            tail = add_per_head(sc[n_keys - DIAG_KEYS:, :], causal)
            return tail if n_keys == DIAG_KEYS else jnp.concatenate([sc[:n_keys - DIAG_KEYS, :], tail], axis=0)

        attend(1, lambda g: ks_ref[0:n_keys, g * LANES:(g + 1) * LANES],
               lambda g: vs_ref[g * V_ROWS:(g + 1) * V_ROWS, 0:n_keys], masked)

    for v in range(n_variants):
        pl.when(lax.div(i, TILES_PER_DIAG) == v)(functools.partial(selected, (v + 1) * DIAG_KEYS))

    attend(2, lambda g: kw_ref[pl.ds(w0, WIN_KEYS), g * LANES:(g + 1) * LANES],
           lambda g: vw_ref[g * V_ROWS:(g + 1) * V_ROWS, pl.ds(w0, WIN_KEYS)], lambda sc: add_per_head(sc, bias_w))

    o_t = [out_ref[hh // HEAD_PACK][:, (hh % HEAD_PACK) * ATT_T:(hh % HEAD_PACK + 1) * ATT_T]
           for hh in range(NSA_HEADS)]
    o_ref[...] = jnp.concatenate(o_t, axis=0).T.astype(o_ref.dtype)


def _attention(q_t, kcb, vcb_t, ks, vs_t, kw, vw_t, gldt, ovt, b, s):
    nq = s // ATT_T
    n_sel = s // SEL_BLOCK
    assert DIAG_KEYS % ATT_T == 0 and s % DIAG_KEYS == 0 and WINDOW % ATT_T == 0
    keys = pl.BlockSpec((s, KV_PAD), lambda r, i: (r, 0))
    vals_t = pl.BlockSpec((NSA_KV_HEADS * V_ROWS, s), lambda r, i: (0, r))
    return pl.pallas_call(
        functools.partial(_attn_body, n_sel=n_sel, n_variants=s // DIAG_KEYS),
        grid=(b, nq),
        in_specs=[
            pl.BlockSpec((NSA_HEADS * HEAD_DIM, ATT_T), lambda r, i: (0, r * nq + i)),
            pl.BlockSpec((None, N_CMP_PAD, KV_PAD), lambda r, i: (r, 0, 0)),
            pl.BlockSpec((None, NSA_KV_HEADS * HEAD_DIM, N_CMP_PAD), lambda r, i: (r, 0, 0)),
            keys, vals_t, keys, vals_t,
            pl.BlockSpec((ATT_T, LANES), lambda r, i: (r * nq + i, 0)),
            pl.BlockSpec(ovt.shape, lambda r, i: (0, 0)),
        ],
        out_specs=pl.BlockSpec((ATT_T, NSA_HEADS * HEAD_DIM), lambda r, i: (r * nq + i, 0)),
        out_shape=jax.ShapeDtypeStruct((b * s, NSA_HEADS * HEAD_DIM), BF16),
        scratch_shapes=[pltpu.VMEM((NSA_HEADS // HEAD_PACK, HEAD_DIM, HEAD_PACK * ATT_T), F32)],
        compiler_params=_params("parallel", "parallel"),
        name="nsa_attention",
    )(q_t, kcb, vcb_t, ks, vs_t, kw, vw_t, gldt, ovt)


CONV_PAD = 8
DT_LANE = 3 * NSA_HEADS


def _split_dot(a, b):
    b0 = b.astype(BF16)
    r1 = b - b0.astype(F32)
    b1 = r1.astype(BF16)
    b2 = (r1 - b1.astype(F32)).astype(BF16)
    return _dot(a, b0) + _dot(a, b1) + _dot(a, b2)


def _split_dot_rhs(a, b):
    a0 = a.astype(BF16)
    r1 = a - a0.astype(F32)
    a1 = r1.astype(BF16)
    a2 = (r1 - a1.astype(F32)).astype(BF16)
    return _dot(a0, b) + _dot(a1, b) + _dot(a2, b)


def _softplus(x):
    return jnp.maximum(x, 0.0) + jnp.log1p(jnp.exp(-jnp.abs(x)))


def _ssd_consts(dt_bias, a_log):
    lanes = np.arange(DT_LANE, DT_LANE + SSM_HEADS)
    dtb = jnp.zeros((1, LANES), F32).at[0, lanes].set(dt_bias)
    alog = jnp.zeros((1, LANES), F32).at[0, lanes].set(a_log)
    expand = np.zeros((LANES, SSM_D_INNER), np.float32)
    for h in range(SSM_HEADS):
        expand[DT_LANE + h, h * SSM_HEAD_DIM:(h + 1) * SSM_HEAD_DIM] = 1.0
    return dtb, alog, jnp.asarray(expand, BF16)


SSD_CHUNKS = 8
SSD_ROWS = SSD_CHUNKS * SSM_CHUNK


def _ssd_body(xbc_ref, z_ref, gldt_ref, cw_ref, cb_ref, dtb_ref, alog_ref, expand_ref, d_ref, nw_ref, o_ref,
              state_ref, xe_ref):
    step = pl.program_id(1)
    L = SSM_CHUNK

    @pl.when(step == 0)
    def _():
        state_ref[...] = jnp.zeros(state_ref.shape, F32)
        xe_ref[0:CONV_PAD, :] = jnp.zeros((CONV_PAD, SSM_CONV_DIM), F32)

    x_raw = xbc_ref[...]
    xe_ref[CONV_PAD:CONV_PAD + SSD_ROWS, :] = x_raw
    acc = jnp.zeros((SSD_ROWS, SSM_CONV_DIM), F32)
    for k in range(SSM_CONV):
        lo = CONV_PAD - (SSM_CONV - 1) + k
        acc = acc + xe_ref[lo:lo + SSD_ROWS, :] * cw_ref[k:k + 1, :]
    xe_ref[0:CONV_PAD, :] = x_raw[SSD_ROWS - CONV_PAD:, :]
    xa = _silu(acc + cb_ref[...])

    row = lax.broadcasted_iota(jnp.int32, (L, L), 0)
    col = lax.broadcasted_iota(jnp.int32, (L, L), 1)
    causal = col <= row
    tri = jnp.where(causal, 1.0, 0.0).astype(BF16)
    lane = lax.broadcasted_iota(jnp.int32, (L, LANES), 1)
    rpg = SSM_HEADS // SSM_GROUPS
    gw = rpg * SSM_HEAD_DIM

    lane128 = lax.broadcasted_iota(jnp.int32, (1, LANES), 1)
    on_dt = (lane128 >= DT_LANE) & (lane128 < DT_LANE + SSM_HEADS)
    dt_all = _softplus(gldt_ref[...] + dtb_ref[...])
    a_all = dt_all * jnp.where(on_dt, -jnp.exp(alog_ref[...]), 0.0)
    z_act = _silu(z_ref[...])

    for j in range(SSD_CHUNKS):
        rows = slice(j * L, (j + 1) * L)
        xs = xa[rows, :SSM_D_INNER]
        bm = xa[rows, SSM_D_INNER:SSM_D_INNER + SSM_BC_DIM]
        cm = xa[rows, SSM_D_INNER + SSM_BC_DIM:]
        a_cs_c = _split_dot(tri, a_all[rows, :])
        a_cs_t = a_cs_c.T[DT_LANE:DT_LANE + SSM_HEADS, :]
        dt = _split_dot_rhs(dt_all[rows, :], expand_ref[...])
        a_cs = _split_dot_rhs(a_cs_c, expand_ref[...])
        a_last = a_cs[L - 1:L, :]

        xdt = xs * dt
        xw = (xdt * jnp.exp(a_last - a_cs)).astype(BF16)
        exp_cs = jnp.exp(a_cs)
        ys = []
        for g in range(SSM_GROUPS):
            bm_g = bm[:, g * SSM_STATE:(g + 1) * SSM_STATE]
            cm_g = cm[:, g * SSM_STATE:(g + 1) * SSM_STATE].astype(BF16)
            cb = _dot_nt(cm_g, bm_g.astype(BF16))
            prev = state_ref[:, g * gw:(g + 1) * gw]
            y_off = _dot(cm_g, prev.astype(BF16)) * exp_cs[:, g * gw:(g + 1) * gw]
            new_state = _dot(bm_g.T.astype(BF16), xw[:, g * gw:(g + 1) * gw])
            state_ref[:, g * gw:(g + 1) * gw] = prev * jnp.exp(a_last[:, g * gw:(g + 1) * gw]) + new_state
            for pair in range(rpg // 2):
                h0 = g * rpg + 2 * pair
                x_pair = xdt[:, h0 * SSM_HEAD_DIM:(h0 + 2) * SSM_HEAD_DIM].astype(BF16)
                halves = []
                for h in (h0, h0 + 1):
                    diff = a_cs[:, h * SSM_HEAD_DIM:h * SSM_HEAD_DIM + 1] - a_cs_t[h:h + 1, :]
                    decay = jnp.exp(jnp.where(causal, diff, NEG_INF))
                    halves.append(_dot((cb * decay).astype(BF16), x_pair))
                ys.append(jnp.where(lane < SSM_HEAD_DIM, halves[0], halves[1])
                          + y_off[:, 2 * pair * SSM_HEAD_DIM:(2 * pair + 2) * SSM_HEAD_DIM])
        y = jnp.concatenate(ys, axis=-1) + xs * d_ref[...]
        y = y * z_act[rows, :]
        outs = []
        for g in range(SSM_GROUPS):
            yg = y[:, g * gw:(g + 1) * gw]
            outs.append(_rms(yg, nw_ref[:, g * gw:(g + 1) * gw]))
        o_ref[rows, :] = jnp.concatenate(outs, axis=-1).astype(o_ref.dtype)


def _ssd(xbc, z, gldt, conv_w, conv_b, dt_bias, a_log, d_skip, norm_w, b, s):
    nc = s // SSD_ROWS
    rows = lambda n: pl.BlockSpec((SSD_ROWS, n), lambda r, c: (r * nc + c, 0))
    full = lambda arr: pl.BlockSpec(arr.shape, lambda r, c: (0,) * arr.ndim)
    consts = (conv_w, conv_b[None, :], *_ssd_consts(dt_bias, a_log), jnp.repeat(d_skip, SSM_HEAD_DIM)[None, :],
              norm_w[None, :])
    return pl.pallas_call(
        _ssd_body,
        grid=(b, nc),
        in_specs=[rows(SSM_CONV_DIM), rows(SSM_D_INNER), rows(LANES)] + [full(a) for a in consts],
        out_specs=rows(SSM_D_INNER),
        out_shape=jax.ShapeDtypeStruct((b * s, SSM_D_INNER), BF16),
        scratch_shapes=[
            pltpu.VMEM((SSM_STATE, SSM_D_INNER), F32),
            pltpu.VMEM((CONV_PAD + SSD_ROWS, SSM_CONV_DIM), F32),
        ],
        compiler_params=_params("parallel", "arbitrary"),
        name="ssd",
    )(xbc, z, gldt, *consts)


def _even_mixer(h, gains, layer, w_in, pe, w1, w2, conv_w, conv_b, dt_bias, a_log, d_skip, norm_w, b, s):
    w_rows, w_cols = _prep_w_in(w_in)
    kvc, ks, kw, z, xbc, gldt, q_t, vs_t, vw_t = _inproj(h, gains, w_rows, w_cols, _position_features(s), layer)
    kcb, vcb_t = _compress(kvc, *_prep_cmp(pe, w1, w2), b, s)
    o_attn = _attention(q_t, kcb, vcb_t, ks, vs_t, kw, vw_t, gldt, _overlap_t(s), b, s)
    o_ssm = _ssd(xbc, z, gldt, conv_w, conv_b, dt_bias, a_log, d_skip, norm_w, b, s)
    return o_attn, o_ssm


def kernel(x, norm_gains, ffn_w_gate, ffn_w_up, ffn_w_down, ev_w_in, ev_w_out, nsa_cmp_pe, nsa_cmp_w1, nsa_cmp_w2,
           ssm_conv_w, ssm_conv_b, ssm_dt_bias, ssm_a_log, ssm_d, ssm_norm_w, od_w_in, od_ln_w, od_ln_b, od_w_s,
           od_b_s, od_w_out):
    b, s, d = x.shape
    depth = norm_gains.shape[0]
    h = x.reshape(b * s, d)
    wg = ffn_w_gate.astype(BF16)
    wu = ffn_w_up.astype(BF16)
    wd = ffn_w_down.astype(BF16)
    od_in = od_w_in.astype(BF16)
    od_out = od_w_out.astype(BF16)
    od_lnw = od_ln_w[:, None, :]
    od_lnb = od_ln_b[:, None, :]
    od_bst = jnp.swapaxes(od_b_s, 1, 2)
    for layer in range(depth):
        i = layer // 2
        if layer % 2 == 0:
            o_attn, o_ssm = _even_mixer(h, norm_gains, layer, ev_w_in[i], nsa_cmp_pe[i], nsa_cmp_w1[i],
                                        nsa_cmp_w2[i], ssm_conv_w[i], ssm_conv_b[i], ssm_dt_bias[i], ssm_a_log[i],
                                        ssm_d[i], ssm_norm_w[i], b, s)
            h = _ffn(h, norm_gains, wg, wu, wd, layer, (o_attn, o_ssm, ev_w_out[i].astype(BF16)))
        else:
            h = _gmlp(h, norm_gains, od_in, od_lnw, od_lnb, od_w_s, od_bst, od_out, layer, i)
            h = _ffn(h, norm_gains, wg, wu, wd, layer)
    return h.reshape(b, s, d)
```
